```python
import math, functools
import jax, jax.numpy as jnp
from jax import lax
import numpy as np

D_MODEL = 1024
BATCH = 1
SEQ = 16384
DEPTH = 1
DEC_BATCH = 128
DEC_SEQ = 1
PAST_LEN = 16384
PAGE_SIZE = 128

MLA_HEADS = 8
Q_RANK = 768
KV_RANK = 256
NOPE_DIM = 64
ROPE_DIM = 32
V_DIM = 64
MLA_QK_DIM = NOPE_DIM + ROPE_DIM
MLA_WIDTH = MLA_HEADS * V_DIM
FOX_HEADS = 8
FOX_HEAD_DIM = 64
FOX_WIDTH = FOX_HEADS * FOX_HEAD_DIM
FORGET_BIAS = 3.0
D_FF = -(-8 * D_MODEL // (3 * 256)) * 256
Q_BLOCK = 128
ROPE_THETA = 10000.0
EPS = 1e-6
NEG_INF = -1e30
MLA_SCALE = MLA_QK_DIM ** -0.5
FOX_SCALE = FOX_HEAD_DIM ** -0.5
IN_SIZES = (Q_RANK, KV_RANK, ROPE_DIM, FOX_WIDTH, FOX_WIDTH, FOX_WIDTH, FOX_HEADS, D_MODEL, D_MODEL)
IN_COLS = sum(IN_SIZES)

kernel_name = "mla_fox_gated_hybrid_step"


def rms_norm(x, g):
    xf = x.astype(jnp.float32)
    y = xf * lax.rsqrt(jnp.mean(xf * xf, axis=-1, keepdims=True) + EPS)
    return (y * g.astype(jnp.float32)).astype(x.dtype)


def apply_rope(x, pos):
    half = ROPE_DIM // 2
    inv = ROPE_THETA ** (-jnp.arange(half, dtype=jnp.float32) / half)
    ang = pos.astype(jnp.float32)[:, None] * inv[None, :]
    if x.ndim == 4:
        ang = ang[:, None, :]
    cos, sin = jnp.cos(ang), jnp.sin(ang)
    xf = x.astype(jnp.float32)
    x1, x2 = xf[..., :half], xf[..., half:]
    return jnp.concatenate([x1 * cos - x2 * sin, x1 * sin + x2 * cos], axis=-1).astype(x.dtype)


def split_columns(proj):
    offsets, acc = [], 0
    for s in IN_SIZES[:-1]:
        acc += s
        offsets.append(acc)
    return jnp.split(proj, offsets, axis=-1)


def mixer_inputs(x, pos, lp):
    B, T = x.shape[0], x.shape[1]
    h = rms_norm(x, lp["norm_attn"])
    c_q, c_kv, kr, fq, fk, fv, ff, ga, gb = split_columns(h @ lp["w_in"])
    q = (rms_norm(c_q, lp["mla_q_norm"]) @ lp["w_uq"]).reshape(B, T, MLA_HEADS, MLA_QK_DIM)
    q_nope = rms_norm(q[..., :NOPE_DIM], lp["mla_qn_nope"])
    q_rope = apply_rope(rms_norm(q[..., NOPE_DIM:], lp["mla_qn_rope"]), pos)
    q_mla = jnp.concatenate([q_nope, q_rope], axis=-1)
    latent = rms_norm(c_kv, lp["mla_kv_norm"])
    k_rope = apply_rope(rms_norm(kr, lp["mla_kn_rope"]), pos)
    fox_q = rms_norm(fq.reshape(B, T, FOX_HEADS, FOX_HEAD_DIM), lp["fox_qn"])
    fox_k = rms_norm(fk.reshape(B, T, FOX_HEADS, FOX_HEAD_DIM), lp["fox_kn"])
    fox_v = fv.reshape(B, T, FOX_HEADS, FOX_HEAD_DIM)
    logf = jax.nn.log_sigmoid((ff + lp["fox_fb"]).astype(jnp.float32)).astype(x.dtype)
    return q_mla, latent, k_rope, fox_q, fox_k, fox_v, logf, ga, gb


def mla_keys(latent, k_rope, lp):
    kv = (latent @ lp["w_ukv"]).reshape(*latent.shape[:-1], MLA_HEADS, NOPE_DIM + V_DIM)
    k_nope = rms_norm(kv[..., :NOPE_DIM], lp["mla_kn_nope"])
    v = kv[..., NOPE_DIM:]
    k_pe = jnp.broadcast_to(k_rope[..., None, :], k_nope.shape[:-1] + (ROPE_DIM,))
    return jnp.concatenate([k_nope, k_pe], axis=-1), v


def attend(q, k, v, bias, scale):
    s = jnp.einsum("bthd,bshd->bhts", q, k).astype(jnp.float32) * scale + bias
    p = jax.nn.softmax(s, axis=-1)
    return jnp.einsum("bhts,bshd->bthd", p.astype(v.dtype), v)


def prompt_mixers(q_mla, latent, k_rope, fox_q, fox_k, fox_v, logf, lp):
    B, S = q_mla.shape[0], q_mla.shape[1]
    pos = jnp.arange(S)
    k_mla, v_mla = mla_keys(latent, k_rope, lp)
    cum = jnp.cumsum(logf.astype(jnp.float32), axis=1).transpose(0, 2, 1)

    def block(i):
        s0 = i * Q_BLOCK
        qp = s0 + jnp.arange(Q_BLOCK)
        causal = qp[:, None] >= pos[None, :]
        qa = lax.dynamic_slice_in_dim(q_mla, s0, Q_BLOCK, axis=1)
        qb = lax.dynamic_slice_in_dim(fox_q, s0, Q_BLOCK, axis=1)
        qcum = lax.dynamic_slice_in_dim(cum, s0, Q_BLOCK, axis=2)
        bias_a = jnp.where(causal, 0.0, NEG_INF)[None, None]
        bias_b = jnp.where(causal[None, None], qcum[..., :, None] - cum[..., None, :], NEG_INF)
        oa = attend(qa, k_mla, v_mla, bias_a, MLA_SCALE)
        ob = attend(qb, fox_k, fox_v, bias_b, FOX_SCALE)
        return oa, ob

    oa, ob = lax.map(block, jnp.arange(S // Q_BLOCK))
    oa = oa.transpose(1, 0, 2, 3, 4).reshape(B, S, MLA_HEADS, V_DIM)
    ob = ob.transpose(1, 0, 2, 3, 4).reshape(B, S, FOX_HEADS, FOX_HEAD_DIM)
    return oa, ob


def sample_mixers(q_mla, latent, k_rope, fox_q, fox_k, fox_v, logf, layer,
                  cache_mla_latent, cache_mla_krope, cache_fox_k, cache_fox_v, cache_fox_logf,
                  page_table, lp):
    T = q_mla.shape[1]
    k_pos = jnp.arange(PAST_LEN + T)
    q_pos = PAST_LEN + jnp.arange(T)
    causal = q_pos[:, None] >= k_pos[None, :]
    bias_a = jnp.where(causal, 0.0, NEG_INF)[None, None]

    def one_seq(args):
        pt, qa, lat_new, kr_new, qb, kb_new, vb_new, lf_new = args
        lat = jnp.concatenate([cache_mla_latent[layer, pt].reshape(PAST_LEN, KV_RANK), lat_new], axis=0)
        kr = jnp.concatenate([cache_mla_krope[layer, pt].reshape(PAST_LEN, ROPE_DIM), kr_new], axis=0)
        k_a, v_a = mla_keys(lat, kr, lp)
        kb = jnp.concatenate([cache_fox_k[layer, pt].reshape(PAST_LEN, FOX_HEADS, FOX_HEAD_DIM), kb_new], axis=0)
        vb = jnp.concatenate([cache_fox_v[layer, pt].reshape(PAST_LEN, FOX_HEADS, FOX_HEAD_DIM), vb_new], axis=0)
        lf = jnp.concatenate([cache_fox_logf[layer, pt].reshape(PAST_LEN, FOX_HEADS), lf_new], axis=0)
        cum = jnp.cumsum(lf.astype(jnp.float32), axis=0).T
        bias_b = jnp.where(causal[None], cum[:, PAST_LEN:, None] - cum[:, None, :], NEG_INF)[None]
        oa = attend(qa[None], k_a[None], v_a[None], bias_a, MLA_SCALE)[0]
        ob = attend(qb[None], kb[None], vb[None], bias_b, FOX_SCALE)[0]
        return oa, ob

    return lax.map(one_seq, (page_table, q_mla, latent, k_rope, fox_q, fox_k, fox_v, logf))


def merge_and_ffn(x, oa, ob, ga, gb, lp):
    B, T = x.shape[0], x.shape[1]
    u_a = oa.reshape(B, T, MLA_WIDTH) @ lp["w_pa"]
    u_b = ob.reshape(B, T, FOX_WIDTH) @ lp["w_pb"]
    mix = jax.nn.sigmoid(ga) * u_a + jax.nn.sigmoid(gb) * u_b
    x = x + mix @ lp["w_o"]
    g, u = jnp.split(rms_norm(x, lp["norm_ffn"]) @ lp["w_gate_up"], 2, axis=-1)
    return x + (jax.nn.silu(g) * u) @ lp["w_down"]


def setup_inputs(seed: int = 0) -> dict:
    key = jax.random.key(seed)
    ks = jax.random.split(key, 28)
    n_pages = PAST_LEN // PAGE_SIZE
    n_used = DEC_BATCH * n_pages
    n_phys = n_used + n_used // 4

    def nrm(k, shape, scale=1.0):
        return scale * jax.random.normal(k, shape, jnp.float32)

    def gain(k, n):
        return 1.0 + 0.05 * nrm(k, (DEPTH, n))

    page_table = jax.random.permutation(ks[7], n_phys)[:n_used].reshape(DEC_BATCH, n_pages).astype(jnp.int32)
    return {
        "x_prompt": nrm(ks[0], (BATCH, SEQ, D_MODEL)),
        "x_sample": nrm(ks[1], (DEC_BATCH, DEC_SEQ, D_MODEL)),
        "cache_mla_latent": nrm(ks[2], (DEPTH, n_phys, PAGE_SIZE, KV_RANK)),
        "cache_mla_krope": nrm(ks[3], (DEPTH, n_phys, PAGE_SIZE, ROPE_DIM)),
        "cache_fox_k": nrm(ks[4], (DEPTH, n_phys, PAGE_SIZE, FOX_HEADS, FOX_HEAD_DIM)),
        "cache_fox_v": nrm(ks[5], (DEPTH, n_phys, PAGE_SIZE, FOX_HEADS, FOX_HEAD_DIM)),
        "cache_fox_logf": jax.nn.log_sigmoid(FORGET_BIAS + nrm(ks[6], (DEPTH, n_phys, PAGE_SIZE, FOX_HEADS))),
        "page_table": page_table,
        "norm_attn": gain(ks[8], D_MODEL),
        "w_in": nrm(ks[9], (DEPTH, D_MODEL, IN_COLS), D_MODEL ** -0.5),
        "mla_q_norm": gain(ks[10], Q_RANK),
        "w_uq": nrm(ks[11], (DEPTH, Q_RANK, MLA_HEADS * MLA_QK_DIM), Q_RANK ** -0.5),
        "mla_kv_norm": gain(ks[12], KV_RANK),
        "w_ukv": nrm(ks[13], (DEPTH, KV_RANK, MLA_HEADS * (NOPE_DIM + V_DIM)), KV_RANK ** -0.5),
        "mla_qn_nope": gain(ks[14], NOPE_DIM),
        "mla_qn_rope": gain(ks[15], ROPE_DIM),
        "mla_kn_nope": gain(ks[16], NOPE_DIM),
        "mla_kn_rope": gain(ks[17], ROPE_DIM),
        "fox_qn": gain(ks[18], FOX_HEAD_DIM),
        "fox_kn": gain(ks[19], FOX_HEAD_DIM),
        "fox_fb": FORGET_BIAS + 0.1 * nrm(ks[20], (DEPTH, FOX_HEADS)),
        "w_pa": nrm(ks[21], (DEPTH, MLA_WIDTH, D_MODEL), MLA_WIDTH ** -0.5),
        "w_pb": nrm(ks[22], (DEPTH, FOX_WIDTH, D_MODEL), FOX_WIDTH ** -0.5),
        "w_o": nrm(ks[23], (DEPTH, D_MODEL, D_MODEL), D_MODEL ** -0.5),
        "norm_ffn": gain(ks[24], D_MODEL),
        "w_gate_up": nrm(ks[25], (DEPTH, D_MODEL, 2 * D_FF), D_MODEL ** -0.5),
        "w_down": nrm(ks[26], (DEPTH, D_FF, D_MODEL), D_FF ** -0.5),
    }


def reference(x_prompt, x_sample, cache_mla_latent, cache_mla_krope, cache_fox_k, cache_fox_v, cache_fox_logf,
              page_table, norm_attn, w_in, mla_q_norm, w_uq, mla_kv_norm, w_ukv, mla_qn_nope, mla_qn_rope,
              mla_kn_nope, mla_kn_rope, fox_qn, fox_kn, fox_fb, w_pa, w_pb, w_o, norm_ffn, w_gate_up, w_down):
    xp, xs = x_prompt, x_sample
    pos_p = jnp.arange(xp.shape[1])
    pos_s = PAST_LEN + jnp.arange(xs.shape[1])
    lat_p, kr_p, fk_p, fv_p, lf_p = [], [], [], [], []
    lat_s, kr_s, fk_s, fv_s, lf_s = [], [], [], [], []
    for layer in range(DEPTH):
        lp = dict(norm_attn=norm_attn[layer], w_in=w_in[layer], mla_q_norm=mla_q_norm[layer], w_uq=w_uq[layer],
                  mla_kv_norm=mla_kv_norm[layer], w_ukv=w_ukv[layer], mla_qn_nope=mla_qn_nope[layer],
                  mla_qn_rope=mla_qn_rope[layer], mla_kn_nope=mla_kn_nope[layer], mla_kn_rope=mla_kn_rope[layer],
                  fox_qn=fox_qn[layer], fox_kn=fox_kn[layer], fox_fb=fox_fb[layer], w_pa=w_pa[layer],
                  w_pb=w_pb[layer], w_o=w_o[layer], norm_ffn=norm_ffn[layer], w_gate_up=w_gate_up[layer],
                  w_down=w_down[layer])
        q_mla, lat, kr, fq, fk, fv, lf, ga, gb = mixer_inputs(xp, pos_p, lp)
        oa, ob = prompt_mixers(q_mla, lat, kr, fq, fk, fv, lf, lp)
        xp = merge_and_ffn(xp, oa, ob, ga, gb, lp)
        lat_p.append(lat); kr_p.append(kr); fk_p.append(fk); fv_p.append(fv); lf_p.append(lf)
        q_mla, lat, kr, fq, fk, fv, lf, ga, gb = mixer_inputs(xs, pos_s, lp)
        oa, ob = sample_mixers(q_mla, lat, kr, fq, fk, fv, lf, layer, cache_mla_latent, cache_mla_krope,
                               cache_fox_k, cache_fox_v, cache_fox_logf, page_table, lp)
        xs = merge_and_ffn(xs, oa, ob, ga, gb, lp)
        lat_s.append(lat); kr_s.append(kr); fk_s.append(fk); fv_s.append(fv); lf_s.append(lf)
    return (xp, xs,
            jnp.stack(lat_p), jnp.stack(kr_p), jnp.stack(fk_p), jnp.stack(fv_p), jnp.stack(lf_p),
            jnp.stack(lat_s), jnp.stack(kr_s), jnp.stack(fk_s), jnp.stack(fv_s), jnp.stack(lf_s))
```

```python
import functools

import numpy as np
import jax
import jax.numpy as jnp
from jax import lax
from jax.experimental import pallas as pl
from jax.experimental.pallas import tpu as pltpu

F32 = jnp.float32
BF16 = jnp.bfloat16

EPS = 1e-6
NEG_INF = -1e30
ROPE_THETA = 10000.0
FORGET_GROUP = 8
LANES = 128
VMEM_LIMIT = 56 * 1024 * 1024

PROJ_ROWS = 256
ATTN_TQ = 256
ATTN_TK = 512
SAMPLE_PAGES_PER_STEP = 4


def _dot(a, b):
    return jnp.dot(a, b, preferred_element_type=F32)


def _dot_nt(a, b):
    return lax.dot_general(a, b, (((1,), (1,)), ((), ())), preferred_element_type=F32)


def _rms(x):
    return x * lax.rsqrt(jnp.mean(x * x, axis=-1, keepdims=True) + EPS)


def _seg_norm(x, seg_mean):
    ms = _dot((x * x).astype(BF16), seg_mean)
    return x * lax.rsqrt(ms + EPS)


def _rope(t, cos_t, sin_a, sin_b):
    half = 16
    return t * cos_t + pltpu.roll(t, LANES - half, 1) * sin_a + pltpu.roll(t, half, 1) * sin_b


def _split3(x):
    hi = x.astype(BF16)
    r = x - hi.astype(F32)
    mid = r.astype(BF16)
    lo = (r - mid.astype(F32)).astype(BF16)
    return hi, mid, lo


def _proj_kernel(x_ref, cos_ref, sina_ref, sinb_ref, gattn_ref, wmain_ref, gq_ref, wuq_ref, gkv_ref,
                 wuk_ref, wuv_ref, gq256_ref, gkn_ref, gkr_ref, gfq_ref, gfk_ref, fb_ref,
                 bq_ref, b64_ref, br_ref, ltri_ref, pcum_ref,
                 qa_ref, ka_ref, va_ref, qf_ref, kf_ref, vf_ref,
                 lat_ref, krt_ref, fkt_ref, fvt_ref, lft_ref,
                 carry_ref, *, q_rank, kv_rank, fox_width, n_pairs):
    @pl.when(pl.program_id(0) == 0)
    def _():
        carry_ref[...] = jnp.zeros_like(carry_ref)

    cos_t, sin_a, sin_b = cos_ref[...], sina_ref[...], sinb_ref[...]
    h = (_rms(x_ref[...]) * gattn_ref[...]).astype(BF16)

    o_kv = q_rank
    o_fq = o_kv + kv_rank
    o_fk = o_fq + fox_width
    o_fv = o_fk + fox_width
    o_kr = o_fv + fox_width
    o_ff = o_kr + LANES

    c_q = _dot(h, wmain_ref[:, 0:o_kv])
    cqn = (_rms(c_q) * gq_ref[...]).astype(BF16)
    q_all = _dot(cqn, wuq_ref[...])
    for j in range(n_pairs):
        blk = q_all[:, 2 * LANES * j:2 * LANES * (j + 1)]
        blk = _seg_norm(blk, bq_ref[...]) * gq256_ref[...]
        rope = _rope(blk[:, LANES:], cos_t, sin_a, sin_b)
        qa_ref[:, 2 * LANES * j:2 * LANES * j + LANES] = blk[:, :LANES].astype(BF16)
        qa_ref[:, 2 * LANES * j + LANES:2 * LANES * (j + 1)] = rope.astype(BF16)

    c_kv = _dot(h, wmain_ref[:, o_kv:o_fq])
    latent = _rms(c_kv) * gkv_ref[...]
    lat_ref[...] = latent
    lat_b = latent.astype(BF16)
    kn = _seg_norm(_dot(lat_b, wuk_ref[...]), b64_ref[...]) * gkn_ref[...]
    va_ref[...] = _dot(lat_b, wuv_ref[...]).astype(BF16)
    kr = _dot(h, wmain_ref[:, o_kr:o_ff])
    kr = _rope(_seg_norm(kr, br_ref[...]) * gkr_ref[...], cos_t, sin_a, sin_b)
    krt_ref[...] = kr.T[0:krt_ref.shape[0], :]
    kr_b = kr.astype(BF16)
    kn_b = kn.astype(BF16)
    for j in range(n_pairs):
        ka_ref[:, 2 * LANES * j:2 * LANES * j + LANES] = kn_b[:, LANES * j:LANES * (j + 1)]
        ka_ref[:, 2 * LANES * j + LANES:2 * LANES * (j + 1)] = kr_b

    fq = _seg_norm(_dot(h, wmain_ref[:, o_fq:o_fk]), b64_ref[...]) * gfq_ref[...]
    fk = _seg_norm(_dot(h, wmain_ref[:, o_fk:o_fv]), b64_ref[...]) * gfk_ref[...]
    fv = _dot(h, wmain_ref[:, o_fv:o_kr])
    fkt_ref[...] = fk.T
    fvt_ref[...] = fv.T
    vf_ref[...] = fv.astype(BF16)

    ff = _dot(h, wmain_ref[:, o_ff:o_ff + LANES]) + fb_ref[...]
    logf = jnp.minimum(ff, 0.0) - jnp.log1p(jnp.exp(-jnp.abs(ff)))
    lane = lax.broadcasted_iota(jnp.int32, logf.shape, 1)
    logf = jnp.where(lane < lft_ref.shape[0], logf, 0.0)
    lft_ref[...] = logf.T[0:lft_ref.shape[0], :]
    l_hi, l_mid, l_lo = _split3(logf)
    ltri = ltri_ref[...]
    cum = carry_ref[...] + (_dot(ltri, l_hi) + _dot(ltri, l_mid) + _dot(ltri, l_lo))
    carry_ref[...] = cum[cum.shape[0] - 1:, :]
    c_hi, c_mid, c_lo = _split3(cum)
    packed = (c_hi.astype(F32) + pltpu.roll(c_mid.astype(F32), FORGET_GROUP, 1)
              + pltpu.roll(c_lo.astype(F32), 2 * FORGET_GROUP, 1)
              + jnp.where(lane == 3 * FORGET_GROUP, 1.0, 0.0))
    aug = _dot(packed.astype(BF16), pcum_ref[...]).astype(BF16)
    fq_b = fq.astype(BF16)
    fk_b = fk.astype(BF16)
    for j in range(n_pairs):
        qf_ref[:, 2 * LANES * j:2 * LANES * j + LANES] = fq_b[:, LANES * j:LANES * (j + 1)]
        qf_ref[:, 2 * LANES * j + LANES:2 * LANES * (j + 1)] = aug[:, :LANES]
        kf_ref[:, 2 * LANES * j:2 * LANES * j + LANES] = fk_b[:, LANES * j:LANES * (j + 1)]
        kf_ref[:, 2 * LANES * j + LANES:2 * LANES * (j + 1)] = aug[:, LANES:]


def _block_diag(sizes_scales, n):
    m = np.zeros((n, n), np.float32)
    o = 0
    for size, scale in sizes_scales:
        m[o:o + size, o:o + size] = scale
        o += size
    return m


def _proj_consts(rows, n_heads, nope, rope_dim, head_dim):
    bq = _block_diag([(nope, 1.0 / nope)] * 2 + [(rope_dim, 1.0 / rope_dim)] * 2, 2 * LANES)
    b64 = _block_diag([(head_dim, 1.0 / head_dim)] * n_heads, n_heads * head_dim)
    br = _block_diag([(rope_dim, 1.0 / rope_dim)] * 2, LANES)
    ltri = np.tril(np.ones((rows, rows), np.float32))
    g = FORGET_GROUP
    pcum = np.zeros((LANES, 2 * LANES), np.float32)
    for h in range(n_heads):
        for part in range(3):
            pcum[part * g + h, g * h + part] = 1.0
            pcum[3 * g, g * h + 3 + part] = 1.0
            pcum[3 * g, LANES + g * h + part] = 1.0
            pcum[part * g + h, LANES + g * h + 3 + part] = -1.0
    return tuple(jnp.asarray(a, BF16) for a in (bq, b64, br, ltri, pcum))


def _rope_tables(pos):
    half = 16
    inv = ROPE_THETA ** (-jnp.arange(half, dtype=F32) / half)
    ang = pos.astype(F32)[:, None] * inv[None, :]
    c, s = jnp.cos(ang), jnp.sin(ang)
    z = jnp.zeros_like(c)
    z64 = jnp.zeros((pos.shape[0], 4 * half), F32)
    return (jnp.concatenate([c, c, c, c, z64], axis=1),
            jnp.concatenate([-s, z, -s, z, z64], axis=1),
            jnp.concatenate([z, s, z, s, z64], axis=1))


def _const_spec(shape, single=False):
    nd = len(shape)
    if single:
        return pl.BlockSpec(shape, lambda *_: (0,) * nd, pipeline_mode=pl.Buffered(1))
    return pl.BlockSpec(shape, lambda *_: (0,) * nd)


def _project(x, pos, wts, rows):
    n, d = x.shape
    cfg = wts["cfg"]
    n_pairs = cfg["heads"] // 2
    fox_width = cfg["heads"] * cfg["head_dim"]
    tables = _rope_tables(pos)
    consts = _proj_consts(rows, cfg["heads"], cfg["nope"], cfg["rope"], cfg["head_dim"])
    row = lambda w: pl.BlockSpec((rows, w), lambda i: (i, 0))
    col = lambda r: pl.BlockSpec((r, rows), lambda i: (0, i))
    weights = (wts["g_attn"], wts["w_main"], wts["g_q"], wts["w_uq"], wts["g_kv"], wts["w_uk"], wts["w_uv"],
               wts["g_q256"], wts["g_kn"], wts["g_kr"], wts["g_fq"], wts["g_fk"], wts["fb"]) + consts
    in_specs = [row(d), row(LANES), row(LANES), row(LANES)] + [_const_spec(w.shape) for w in weights]
    pair_w = 2 * LANES * n_pairs
    out_shape = (
        jax.ShapeDtypeStruct((n, pair_w), BF16), jax.ShapeDtypeStruct((n, pair_w), BF16),
        jax.ShapeDtypeStruct((n, fox_width), BF16),
        jax.ShapeDtypeStruct((n, pair_w), BF16), jax.ShapeDtypeStruct((n, pair_w), BF16),
        jax.ShapeDtypeStruct((n, fox_width), BF16),
        jax.ShapeDtypeStruct((n, cfg["kv_rank"]), F32),
        jax.ShapeDtypeStruct((cfg["rope"], n), F32),
        jax.ShapeDtypeStruct((fox_width, n), F32), jax.ShapeDtypeStruct((fox_width, n), F32),
        jax.ShapeDtypeStruct((cfg["heads"], n), F32),
    )
    out_specs = (row(pair_w), row(pair_w), row(fox_width), row(pair_w), row(pair_w), row(fox_width),
                 row(cfg["kv_rank"]), col(cfg["rope"]), col(fox_width), col(fox_width), col(cfg["heads"]))
    kern = functools.partial(_proj_kernel, q_rank=cfg["q_rank"], kv_rank=cfg["kv_rank"],
                             fox_width=fox_width, n_pairs=n_pairs)
    return pl.pallas_call(
        kern, grid=(n // rows,), in_specs=in_specs, out_specs=out_specs, out_shape=out_shape,
        scratch_shapes=[pltpu.VMEM((1, LANES), F32)],
        compiler_params=pltpu.CompilerParams(dimension_semantics=("arbitrary",), vmem_limit_bytes=VMEM_LIMIT),
        name="proj",
    )(x, *tables, *weights)


def _attn_kernel(q_ref, k_ref, v_ref, o_ref, *, tq, tk, head_dim, aug_width, aug_per_pair):
    pair = pl.program_id(0)
    qi = pl.program_id(1)
    q = q_ref[...]
    lane = lax.broadcasted_iota(jnp.int32, (1, q.shape[1]), 1)
    qm = []
    for half in range(2):
        if aug_per_pair:
            a0 = LANES + aug_width * half
        else:
            a0 = LANES + aug_width * (2 * pair + half)
        keep = ((lane >= head_dim * half) & (lane < head_dim * (half + 1))) | ((lane >= a0) & (lane < a0 + aug_width))
        qm.append(jnp.where(keep, q, jnp.zeros_like(q)))

    def step(kb, carry, masked):
        start = pl.multiple_of(kb * tk, tk)
        k = k_ref[pl.ds(start, tk), :]
        v = v_ref[pl.ds(start, tk), :]
        if masked:
            row = qi * tq + lax.broadcasted_iota(jnp.int32, (tq, tk), 0)
            col = kb * tk + lax.broadcasted_iota(jnp.int32, (tq, tk), 1)
            causal = col <= row
        out = []
        for half in range(2):
            m, l, acc = carry[half]
            s = _dot_nt(qm[half], k)
            if masked:
                s = jnp.where(causal, s, NEG_INF)
            m_new = jnp.maximum(m, jnp.max(s, axis=1, keepdims=True))
            alpha = jnp.exp(m - m_new)
            p = jnp.exp(s - m_new)
            l = alpha * l + jnp.sum(p, axis=1, keepdims=True)
            acc = alpha * acc + _dot(p.astype(BF16), v)
            out.append((m_new, l, acc))
        return tuple(out)

    init = tuple((jnp.full((tq, 1), NEG_INF, F32), jnp.zeros((tq, 1), F32), jnp.zeros((tq, LANES), F32))
                 for _ in range(2))
    n_full = (qi * tq) // tk
    carry = lax.fori_loop(0, n_full, lambda kb, c: step(kb, c, False), init)
    carry = step(n_full, carry, True)
    (_, l_e, acc_e), (_, l_o, acc_o) = carry
    lane_o = lax.broadcasted_iota(jnp.int32, (1, LANES), 1)
    o_ref[...] = jnp.where(lane_o < head_dim, acc_e / l_e, acc_o / l_o).astype(o_ref.dtype)


def _attention(q_cat, k_cat, v, *, head_dim, aug_width, aug_per_pair, name):
    s = q_cat.shape[0]
    n_pairs = q_cat.shape[1] // (2 * LANES)
    tq, tk = min(ATTN_TQ, s), min(ATTN_TK, s)
    kern = functools.partial(_attn_kernel, tq=tq, tk=tk, head_dim=head_dim, aug_width=aug_width,
                             aug_per_pair=aug_per_pair)
    return pl.pallas_call(
        kern, grid=(n_pairs, s // tq),
        in_specs=[pl.BlockSpec((tq, 2 * LANES), lambda j, i: (i, j)),
                  pl.BlockSpec((s, 2 * LANES), lambda j, i: (0, j)),
                  pl.BlockSpec((s, LANES), lambda j, i: (0, j))],
        out_specs=pl.BlockSpec((tq, LANES), lambda j, i: (i, j)),
        out_shape=jax.ShapeDtypeStruct((s, n_pairs * LANES), BF16),
        compiler_params=pltpu.CompilerParams(dimension_semantics=("arbitrary", "arbitrary"),
                                             vmem_limit_bytes=VMEM_LIMIT),
        name=name,
    )(q_cat, k_cat, v)


def _sample_kernel(pt_ref, qn_ref, qr_ref, qf_ref, latn_ref, krn_ref, fkn_ref, fvn_ref, lfn_ref,
                   wuk_ref, wuv_ref, gkn_ref, seg_ref, *rest, n_heads, head_dim, pages_per_step):
    page_refs = rest[:5 * pages_per_step]
    oa_ref, of_ref = rest[5 * pages_per_step:5 * pages_per_step + 2]
    a_ref, qfb_ref, ma_ref, la_ref, acca_ref, mf_ref, lf_ref, accf_ref, carry_ref = rest[5 * pages_per_step + 2:]
    b = pl.program_id(0)
    step = pl.program_id(1)
    hp = a_ref.shape[0]
    width = n_heads * head_dim

    def pad_heads(x):
        return jnp.concatenate([x, jnp.zeros((hp - n_heads, x.shape[1]), F32)], axis=0).astype(BF16)

    def page(lat, kr_t, fk_t, fv_t, lf_t, valid):
        w = lat.shape[0]
        lat_b = lat.astype(BF16)
        kraw = _dot(lat_b, wuk_ref[...])
        msq = _dot_nt(seg_ref[...], (kraw * kraw).astype(BF16))[:n_heads]
        numer = _dot_nt(a_ref[...], lat_b)[:n_heads]
        rope = _dot(qr_ref[...], kr_t.astype(BF16))[:n_heads]
        s_a = numer * lax.rsqrt(msq + EPS) + rope
        s_f = _dot(qfb_ref[...], fk_t.astype(BF16))[:n_heads]
        if valid is not None:
            lf_t = jnp.where(valid, lf_t, 0.0)
        hi = lf_t.astype(BF16).astype(F32)
        parts = jnp.concatenate([hi, lf_t - hi], axis=0).astype(BF16)
        jj = lax.broadcasted_iota(jnp.int32, (w, w), 0)
        ss = lax.broadcasted_iota(jnp.int32, (w, w), 1)
        later = jnp.where(jj > ss, 1.0, 0.0).astype(BF16)
        suf = _dot(parts, later)
        carry = carry_ref[...][:, 0:1]
        s_f = s_f + (carry + suf[:n_heads] + suf[n_heads:])
        carry_ref[...] = jnp.broadcast_to(carry + jnp.sum(lf_t, axis=1, keepdims=True), carry_ref.shape)
        if valid is not None:
            s_a = jnp.where(valid, s_a, NEG_INF)
            s_f = jnp.where(valid, s_f, NEG_INF)

        m_old = ma_ref[...][:, 0:1]
        m_new = jnp.maximum(m_old, jnp.max(s_a, axis=1, keepdims=True))
        alpha = jnp.exp(m_old - m_new)
        p = jnp.exp(s_a - m_new)
        la_ref[...] = alpha * la_ref[...] + jnp.sum(p, axis=1, keepdims=True)
        ma_ref[...] = jnp.broadcast_to(m_new, ma_ref.shape)
        acca_ref[...] = alpha * acca_ref[...] + _dot(pad_heads(p), lat_b)[:n_heads]

        m_old = mf_ref[...][:, 0:1]
        m_new = jnp.maximum(m_old, jnp.max(s_f, axis=1, keepdims=True))
        alpha = jnp.exp(m_old - m_new)
        p = jnp.exp(s_f - m_new)
        lf_ref[...] = alpha * lf_ref[...] + jnp.sum(p, axis=1, keepdims=True)
        mf_ref[...] = jnp.broadcast_to(m_new, mf_ref.shape)
        for h in range(n_heads):
            rows = slice(h * head_dim, (h + 1) * head_dim)
            accf_ref[rows, 0:w] = alpha[h:h + 1, :] * accf_ref[rows, 0:w] + p[h:h + 1, :] * fv_t[rows, :]

    @pl.when((step == 0) & (b == 0))
    def _():
        of_ref[...] = jnp.zeros_like(of_ref)

    @pl.when(step == 0)
    def _():
        head = lax.broadcasted_iota(jnp.int32, (hp, width), 0)
        seg = lax.broadcasted_iota(jnp.int32, (hp, width), 1) // head_dim
        own = head == seg
        qg = qn_ref[0].astype(F32) * gkn_ref[...]
        q_bd = jnp.where(own, jnp.broadcast_to(qg, (hp, width)), 0.0).astype(BF16)
        a_ref[...] = _dot_nt(q_bd, wuk_ref[...]).astype(BF16)
        qfb_ref[...] = jnp.where(own, jnp.broadcast_to(qf_ref[0].astype(F32), (hp, width)), 0.0).astype(BF16)
        ma_ref[...] = jnp.full_like(ma_ref, NEG_INF)
        mf_ref[...] = jnp.full_like(mf_ref, NEG_INF)
        la_ref[...] = jnp.zeros_like(la_ref)
        lf_ref[...] = jnp.zeros_like(lf_ref)
        acca_ref[...] = jnp.zeros_like(acca_ref)
        accf_ref[...] = jnp.zeros_like(accf_ref)
        carry_ref[...] = jnp.zeros_like(carry_ref)
        nb = latn_ref.shape[0]
        valid = lax.broadcasted_iota(jnp.int32, (1, nb), 1) == b
        page(latn_ref[...], krn_ref[...], fkn_ref[...], fvn_ref[...], lfn_ref[...], valid)

    for i in range(pages_per_step):
        lat_p, kr_p, fk_p, fv_p, lf_p = page_refs[5 * i:5 * i + 5]
        page(lat_p[...], kr_p[...], fk_p[...], fv_p[...], lf_p[...], None)

    @pl.when(step == pl.num_programs(1) - 1)
    def _():
        ctx = acca_ref[...] / la_ref[...][:, 0:1]
        o_all = _dot(pad_heads(ctx), wuv_ref[...])[:n_heads]
        head = lax.broadcasted_iota(jnp.int32, o_all.shape, 0)
        seg = lax.broadcasted_iota(jnp.int32, o_all.shape, 1) // head_dim
        oa_ref[0] = jnp.sum(jnp.where(head == seg, o_all, 0.0), axis=0, keepdims=True)
        sums = jnp.sum(accf_ref[...], axis=1, keepdims=True)
        inv_l = 1.0 / lf_ref[...][:, 0:1]
        col = jnp.concatenate([sums[h * head_dim:(h + 1) * head_dim] * inv_l[h:h + 1, :]
                               for h in range(n_heads)], axis=0)
        lane = lax.broadcasted_iota(jnp.int32, of_ref.shape, 1)
        of_ref[...] = jnp.where(lane == b, col, of_ref[...])


def _sample_attention(qn, qr, qf, lat_new, kr_new_t, fk_new_t, fv_new_t, lf_new_t, wts,
                      cache_lat, cache_kr_t, cache_fk_t, cache_fv_t, cache_lf_t, page_table):
    cfg = wts["cfg"]
    nb, n_pages = page_table.shape
    page_size = cache_lat.shape[2]
    n_heads, head_dim = cfg["heads"], cfg["head_dim"]
    width = n_heads * head_dim
    hp = 16
    pps = min(SAMPLE_PAGES_PER_STEP, n_pages)
    n_steps = n_pages // pps
    seg = np.zeros((hp, width), np.float32)
    for h in range(n_heads):
        seg[h, h * head_dim:(h + 1) * head_dim] = 1.0 / head_dim
    seg = jnp.asarray(seg, BF16)
    qr16 = jnp.concatenate([qr, jnp.zeros((nb, hp - n_heads, qr.shape[2]), qr.dtype)], axis=1)

    per_seq = lambda shape: pl.BlockSpec((None,) + shape, lambda b, s, pt: (b,) + (0,) * len(shape))

    def page_spec(shape, i):
        def idx(b, s, pt):
            return (0, pt[b * n_pages + (n_pages - 1 - (s * pps + i))]) + (0,) * len(shape)
        return pl.BlockSpec((None, None) + shape, idx)

    consts = (lat_new, kr_new_t, fk_new_t, fv_new_t, lf_new_t, wts["w_uk"], wts["w_uv"], wts["g_kn"], seg)
    in_specs = [pl.BlockSpec((1, 1, width), lambda b, s, pt: (b, 0, 0)),
                per_seq(qr16.shape[1:]),
                pl.BlockSpec((1, 1, width), lambda b, s, pt: (b, 0, 0))]
    in_specs += [_const_spec(c.shape) for c in consts]
    page_args = []
    for i in range(pps):
        for arr in (cache_lat, cache_kr_t, cache_fk_t, cache_fv_t, cache_lf_t):
            in_specs.append(page_spec(arr.shape[2:], i))
            page_args.append(arr)
    kern = functools.partial(_sample_kernel, n_heads=n_heads, head_dim=head_dim, pages_per_step=pps)
    grid_spec = pltpu.PrefetchScalarGridSpec(
        num_scalar_prefetch=1, grid=(nb, n_steps), in_specs=in_specs,
        out_specs=(pl.BlockSpec((1, 1, width), lambda b, s, pt: (b, 0, 0)),
                   pl.BlockSpec((width, nb), lambda b, s, pt: (0, 0))),
        scratch_shapes=[pltpu.VMEM((hp, cfg["kv_rank"]), BF16), pltpu.VMEM((hp, width), BF16),
                        pltpu.VMEM((n_heads, LANES), F32), pltpu.VMEM((n_heads, LANES), F32),
                        pltpu.VMEM((n_heads, cfg["kv_rank"]), F32),
                        pltpu.VMEM((n_heads, LANES), F32), pltpu.VMEM((n_heads, LANES), F32),
                        pltpu.VMEM((width, max(page_size, nb)), F32),
                        pltpu.VMEM((n_heads, LANES), F32)])
    oa, of_t = pl.pallas_call(
        kern, grid_spec=grid_spec,
        out_shape=(jax.ShapeDtypeStruct((nb, 1, width), F32), jax.ShapeDtypeStruct((width, nb), F32)),
        compiler_params=pltpu.CompilerParams(dimension_semantics=("arbitrary", "arbitrary"),
                                             vmem_limit_bytes=VMEM_LIMIT),
        name="sample_attn",
    )(page_table.reshape(-1), qn, qr16, qf, *consts, *page_args)
    return oa.reshape(nb, width), of_t.T


def _merge_kernel(x_ref, oa_ref, ob_ref, gattn_ref, wgab_ref, wpa_ref, wpb_ref, wo_ref, gffn_ref,
                  wgu_ref, wdn_ref, y_ref):
    x = x_ref[...]
    d = x.shape[1]
    h = (_rms(x) * gattn_ref[...]).astype(BF16)
    gates = _dot(h, wgab_ref[...])
    u_a = _dot(oa_ref[...].astype(BF16), wpa_ref[...])
    u_b = _dot(ob_ref[...].astype(BF16), wpb_ref[...])
    mix = jax.nn.sigmoid(gates[:, :d]) * u_a + jax.nn.sigmoid(gates[:, d:]) * u_b
    x1 = x + _dot(mix.astype(BF16), wo_ref[...])
    h2 = (_rms(x1) * gffn_ref[...]).astype(BF16)
    gu = _dot(h2, wgu_ref[...])
    d_ff = gu.shape[1] // 2
    g, u = gu[:, :d_ff], gu[:, d_ff:]
    y_ref[...] = x1 + _dot((g * jax.nn.sigmoid(g) * u).astype(BF16), wdn_ref[...])


def _merge(x, oa, ob, wts, rows):
    n, d = x.shape
    row = lambda w: pl.BlockSpec((rows, w), lambda i: (i, 0))
    weights = (wts["g_attn"], wts["w_gab"], wts["w_pa"], wts["w_pb"], wts["w_o"], wts["g_ffn"],
               wts["w_gu"], wts["w_dn"])
    return pl.pallas_call(
        _merge_kernel, grid=(n // rows,),
        in_specs=[row(d), row(oa.shape[1]), row(ob.shape[1])] + [_const_spec(w.shape, single=True) for w in weights],
        out_specs=row(d), out_shape=jax.ShapeDtypeStruct((n, d), F32),
        compiler_params=pltpu.CompilerParams(dimension_semantics=("arbitrary",), vmem_limit_bytes=VMEM_LIMIT),
        name="merge_ffn",
    )(x, oa, ob, *weights)


def _prepare(layer, cfg, norm_attn, w_in, mla_q_norm, w_uq, mla_kv_norm, w_ukv, mla_qn_nope, mla_qn_rope,
             mla_kn_nope, mla_kn_rope, fox_qn, fox_kn, fox_fb, w_pa, w_pb, w_o, norm_ffn, w_gate_up, w_down):
    d = w_in.shape[1]
    heads, nope, rope, v_dim, hd = cfg["heads"], cfg["nope"], cfg["rope"], cfg["v_dim"], cfg["head_dim"]
    q_rank, kv_rank = cfg["q_rank"], cfg["kv_rank"]
    fw = heads * hd
    sizes = (q_rank, kv_rank, rope, fw, fw, fw, heads, d, d)
    offs = np.cumsum((0,) + sizes)
    w = w_in[layer]
    part = lambda i: w[:, offs[i]:offs[i + 1]]
    zeros = lambda c: jnp.zeros((d, c), w.dtype)
    w_main = jnp.concatenate([part(0), part(1), part(3), part(4), part(5),
                              part(2), part(2), zeros(LANES - 2 * rope),
                              part(6), zeros(LANES - heads)], axis=1).astype(BF16)
    w_gab = w[:, offs[7]:offs[9]].astype(BF16)
    wq = w_uq[layer].reshape(q_rank, heads, nope + rope)
    n_pairs = heads // 2
    wq_nope = wq[:, :, :nope].reshape(q_rank, n_pairs, 2 * nope)
    wq_rope = wq[:, :, nope:].reshape(q_rank, n_pairs, 2 * rope)
    wq_pad = jnp.zeros((q_rank, n_pairs, 2 * LANES - 2 * nope - 2 * rope), w.dtype)
    w_uq_p = jnp.concatenate([wq_nope, wq_rope, wq_pad], axis=2).reshape(q_rank, n_pairs * 2 * LANES).astype(BF16)
    wkv = w_ukv[layer].reshape(kv_rank, heads, nope + v_dim)
    w_uk = wkv[:, :, :nope].reshape(kv_rank, heads * nope).astype(BF16)
    w_uv = wkv[:, :, nope:].reshape(kv_rank, heads * v_dim).astype(BF16)
    mla_scale = float(nope + rope) ** -0.5
    fox_scale = float(hd) ** -0.5
    gn, gr = mla_qn_nope[layer], mla_qn_rope[layer]
    g_q256 = jnp.concatenate([gn, gn, gr, gr, jnp.zeros((2 * LANES - 2 * nope - 2 * rope,), F32)]) * mla_scale
    gkr = mla_kn_rope[layer]
    g_kr = jnp.concatenate([gkr, gkr, jnp.zeros((LANES - 2 * rope,), F32)])
    fb = jnp.concatenate([fox_fb[layer], jnp.zeros((LANES - heads,), F32)])
    r2 = lambda v: v.reshape(1, -1).astype(F32)
    return dict(
        cfg=cfg, g_attn=r2(norm_attn[layer]), w_main=w_main, w_gab=w_gab, g_q=r2(mla_q_norm[layer]), w_uq=w_uq_p,
        g_kv=r2(mla_kv_norm[layer]), w_uk=w_uk, w_uv=w_uv, g_q256=r2(g_q256),
        g_kn=r2(jnp.tile(mla_kn_nope[layer], heads)), g_kr=r2(g_kr),
        g_fq=r2(jnp.tile(fox_qn[layer], heads) * fox_scale), g_fk=r2(jnp.tile(fox_kn[layer], heads)), fb=r2(fb),
        w_pa=w_pa[layer].astype(BF16), w_pb=w_pb[layer].astype(BF16), w_o=w_o[layer].astype(BF16),
        g_ffn=r2(norm_ffn[layer]), w_gu=w_gate_up[layer].astype(BF16), w_dn=w_down[layer].astype(BF16))


def kernel(x_prompt, x_sample, cache_mla_latent, cache_mla_krope, cache_fox_k, cache_fox_v, cache_fox_logf, page_table, norm_attn, w_in, mla_q_norm, w_uq, mla_kv_norm, w_ukv, mla_qn_nope, mla_qn_rope, mla_kn_nope, mla_kn_rope, fox_qn, fox_kn, fox_fb, w_pa, w_pb, w_o, norm_ffn, w_gate_up, w_down):
    batch, seq, d = x_prompt.shape
    nb, dec_seq, _ = x_sample.shape
    depth, n_phys, page_size, kv_rank = cache_mla_latent.shape
    heads, hd = cache_fox_k.shape[3], cache_fox_k.shape[4]
    rope = cache_mla_krope.shape[3]
    nope = mla_qn_nope.shape[1]
    assert batch == 1 and dec_seq == 1 and depth == 1, "kernel supports one prompt sequence, one new token, one layer"
    assert w_uq.shape[2] == heads * (nope + rope) and 2 * nope == LANES and 4 * rope == LANES and hd == nope
    cfg = dict(heads=heads, head_dim=hd, nope=nope, rope=rope, v_dim=w_ukv.shape[2] // heads - nope,
               q_rank=w_uq.shape[1], kv_rank=kv_rank)
    past_len = page_table.shape[1] * page_size
    layer = 0
    wts = _prepare(layer, cfg, norm_attn, w_in, mla_q_norm, w_uq, mla_kv_norm, w_ukv, mla_qn_nope, mla_qn_rope,
                   mla_kn_nope, mla_kn_rope, fox_qn, fox_kn, fox_fb, w_pa, w_pb, w_o, norm_ffn, w_gate_up, w_down)
    fw = heads * hd
    n_pairs = heads // 2

    xp = x_prompt.reshape(seq, d)
    rows_p = min(PROJ_ROWS, seq)
    qa, ka, va, qf, kf, vf, lat_p, krt_p, fkt_p, fvt_p, lft_p = _project(xp, jnp.arange(seq), wts, rows_p)
    oa_p = _attention(qa, ka, va, head_dim=nope, aug_width=rope, aug_per_pair=True, name="attn_mla")
    ob_p = _attention(qf, kf, vf, head_dim=hd, aug_width=FORGET_GROUP, aug_per_pair=False, name="attn_fox")
    y_p = _merge(xp, oa_p, ob_p, wts, rows_p).reshape(batch, seq, d)

    xs = x_sample.reshape(nb, d)
    pos_s = jnp.full((nb,), past_len, jnp.int32)
    qa_s, _, _, qf_s, _, _, lat_s, krt_s, fkt_s, fvt_s, lft_s = _project(xs, pos_s, wts, nb)
    qa3 = qa_s.reshape(nb, n_pairs, 2 * LANES)
    qn = qa3[:, :, :LANES].reshape(nb, 1, fw)
    qr = qa3[:, :, LANES:LANES + 2 * rope].reshape(nb, heads, rope)
    qfn = qf_s.reshape(nb, n_pairs, 2 * LANES)[:, :, :LANES].reshape(nb, 1, fw)
    cache_kr_t = jnp.transpose(cache_mla_krope, (0, 1, 3, 2))
    cache_fk_t = jnp.transpose(cache_fox_k, (0, 1, 3, 4, 2)).reshape(depth, n_phys, fw, page_size)
    cache_fv_t = jnp.transpose(cache_fox_v, (0, 1, 3, 4, 2)).reshape(depth, n_phys, fw, page_size)
    cache_lf_t = jnp.transpose(cache_fox_logf, (0, 1, 3, 2))
    oa_s, ob_s = _sample_attention(qn, qr, qfn, lat_s, krt_s, fkt_s, fvt_s, lft_s, wts,
                                   cache_mla_latent, cache_kr_t, cache_fk_t, cache_fv_t, cache_lf_t, page_table)
    y_s = _merge(xs, oa_s, ob_s, wts, nb).reshape(nb, dec_seq, d)

    def cache_outputs(lat, krt, fkt, fvt, lft, lead):
        n = lat.shape[0]
        shape = lambda *tail: (depth,) + lead + tail
        return (lat.reshape(shape(kv_rank)),
                krt.T.reshape(shape(rope)),
                jnp.transpose(fkt.reshape(heads, hd, n), (2, 0, 1)).reshape(shape(heads, hd)),
                jnp.transpose(fvt.reshape(heads, hd, n), (2, 0, 1)).reshape(shape(heads, hd)),
                lft.T.reshape(shape(heads)))

    return (y_p, y_s) + cache_outputs(lat_p, krt_p, fkt_p, fvt_p, lft_p, (batch, seq)) \
        + cache_outputs(lat_s, krt_s, fkt_s, fvt_s, lft_s, (nb, dec_seq))
```

```python
import functools

import numpy as np
import jax
import jax.numpy as jnp
from jax import lax
from jax.experimental import pallas as pl
from jax.experimental.pallas import tpu as pltpu

F32 = jnp.float32
BF16 = jnp.bfloat16

EPS = 1e-6
NEG_INF = -1e30
ROPE_THETA = 10000.0
LOG2E = 1.4426950408889634
FORGET_GROUP = 8
LANES = 128
VMEM_LIMIT = 56 * 1024 * 1024

PROJ_ROWS = 256
ATTN_TQ = 256
ATTN_TK = 1024
SAMPLE_PAGES_PER_STEP = 8


def _dot(a, b):
    return jnp.dot(a, b, preferred_element_type=F32)


def _dot_nt(a, b):
    return lax.dot_general(a, b, (((1,), (1,)), ((), ())), preferred_element_type=F32)


def _rms(x):
    return x * lax.rsqrt(jnp.mean(x * x, axis=-1, keepdims=True) + EPS)


def _seg_norm(x, seg_mean):
    ms = _dot((x * x).astype(BF16), seg_mean)
    return x * lax.rsqrt(ms + EPS)


def _rope(t, cos_t, sin_a, sin_b):
    half = 16
    return t * cos_t + pltpu.roll(t, LANES - half, 1) * sin_a + pltpu.roll(t, half, 1) * sin_b


def _split3(x):
    hi = x.astype(BF16)
    r = x - hi.astype(F32)
    mid = r.astype(BF16)
    lo = (r - mid.astype(F32)).astype(BF16)
    return hi, mid, lo


def _proj_kernel(x_ref, cos_ref, sina_ref, sinb_ref, gattn_ref, wmain_ref, gq_ref, wuq_ref, gkv_ref,
                 wuk_ref, wuvt_ref, wfvt_ref, gq256_ref, gkn_ref, gkr_ref, gfq_ref, gfk_ref, fb_ref,
                 bq_ref, b64_ref, br_ref, ltri_ref, pcum_ref,
                 qa_ref, ka_ref, vat_ref, qf_ref, kf_ref, vft_ref,
                 lat_ref, krt_ref, fkt_ref, fvt_ref, lft_ref,
                 carry_ref, *, q_rank, kv_rank, fox_width, n_pairs):
    @pl.when(pl.program_id(0) == 0)
    def _():
        carry_ref[...] = jnp.zeros_like(carry_ref)

    cos_t, sin_a, sin_b = cos_ref[...], sina_ref[...], sinb_ref[...]
    h = (_rms(x_ref[...]) * gattn_ref[...]).astype(BF16)

    o_kv = q_rank
    o_fq = o_kv + kv_rank
    o_fk = o_fq + fox_width
    o_kr = o_fk + fox_width
    o_ff = o_kr + LANES

    c_q = _dot(h, wmain_ref[:, 0:o_kv])
    cqn = (_rms(c_q) * gq_ref[...]).astype(BF16)
    q_all = _dot(cqn, wuq_ref[...])
    for j in range(n_pairs):
        blk = q_all[:, 2 * LANES * j:2 * LANES * (j + 1)]
        blk = _seg_norm(blk, bq_ref[...]) * gq256_ref[...]
        rope = _rope(blk[:, LANES:], cos_t, sin_a, sin_b)
        qa_ref[:, 2 * LANES * j:2 * LANES * j + LANES] = blk[:, :LANES].astype(BF16)
        qa_ref[:, 2 * LANES * j + LANES:2 * LANES * (j + 1)] = rope.astype(BF16)

    c_kv = _dot(h, wmain_ref[:, o_kv:o_fq])
    latent = _rms(c_kv) * gkv_ref[...]
    lat_ref[...] = latent
    lat_b = latent.astype(BF16)
    kn = _seg_norm(_dot(lat_b, wuk_ref[...]), b64_ref[...]) * gkn_ref[...]
    vat_ref[...] = _dot_nt(wuvt_ref[...], lat_b).astype(BF16)
    kr = _dot(h, wmain_ref[:, o_kr:o_ff])
    kr = _rope(_seg_norm(kr, br_ref[...]) * gkr_ref[...], cos_t, sin_a, sin_b)
    krt_ref[...] = kr.T[0:krt_ref.shape[0], :]
    kr_b = kr.astype(BF16)
    kn_b = kn.astype(BF16)
    for j in range(n_pairs):
        ka_ref[:, 2 * LANES * j:2 * LANES * j + LANES] = kn_b[:, LANES * j:LANES * (j + 1)]
        ka_ref[:, 2 * LANES * j + LANES:2 * LANES * (j + 1)] = kr_b

    fq = _seg_norm(_dot(h, wmain_ref[:, o_fq:o_fk]), b64_ref[...]) * gfq_ref[...]
    fk = _seg_norm(_dot(h, wmain_ref[:, o_fk:o_kr]), b64_ref[...]) * gfk_ref[...]
    fkt_ref[...] = fk.T
    fv_t = _dot_nt(wfvt_ref[...], h)
    fvt_ref[...] = fv_t
    vft_ref[...] = fv_t.astype(BF16)

    ff = _dot(h, wmain_ref[:, o_ff:o_ff + LANES]) + fb_ref[...]
    logf = jnp.minimum(ff, 0.0) - jnp.log1p(jnp.exp(-jnp.abs(ff)))
    lane = lax.broadcasted_iota(jnp.int32, logf.shape, 1)
    logf = jnp.where(lane < lft_ref.shape[0], logf, 0.0)
    lft_ref[...] = logf.T[0:lft_ref.shape[0], :]
    l_hi, l_mid, l_lo = _split3(logf)
    ltri = ltri_ref[...]
    cum = carry_ref[...] + (_dot(ltri, l_hi) + _dot(ltri, l_mid) + _dot(ltri, l_lo))
    carry_ref[...] = cum[cum.shape[0] - 1:, :]
    c_hi, c_mid, c_lo = _split3(cum * LOG2E)
    packed = (c_hi.astype(F32) + pltpu.roll(c_mid.astype(F32), FORGET_GROUP, 1)
              + pltpu.roll(c_lo.astype(F32), 2 * FORGET_GROUP, 1)
              + jnp.where(lane == 3 * FORGET_GROUP, 1.0, 0.0))
    aug = _dot(packed.astype(BF16), pcum_ref[...]).astype(BF16)
    fq_b = fq.astype(BF16)
    fk_b = fk.astype(BF16)
    for j in range(n_pairs):
        qf_ref[:, 2 * LANES * j:2 * LANES * j + LANES] = fq_b[:, LANES * j:LANES * (j + 1)]
        qf_ref[:, 2 * LANES * j + LANES:2 * LANES * (j + 1)] = aug[:, :LANES]
        kf_ref[:, 2 * LANES * j:2 * LANES * j + LANES] = fk_b[:, LANES * j:LANES * (j + 1)]
        kf_ref[:, 2 * LANES * j + LANES:2 * LANES * (j + 1)] = aug[:, LANES:]


def _block_diag(sizes_scales, n):
    m = np.zeros((n, n), np.float32)
    o = 0
    for size, scale in sizes_scales:
        m[o:o + size, o:o + size] = scale
        o += size
    return m


def _proj_consts(rows, n_heads, nope, rope_dim, head_dim):
    bq = _block_diag([(nope, 1.0 / nope)] * 2 + [(rope_dim, 1.0 / rope_dim)] * 2, 2 * LANES)
    b64 = _block_diag([(head_dim, 1.0 / head_dim)] * n_heads, n_heads * head_dim)
    br = _block_diag([(rope_dim, 1.0 / rope_dim)] * 2, LANES)
    ltri = np.tril(np.ones((rows, rows), np.float32))
    g = FORGET_GROUP
    pcum = np.zeros((LANES, 2 * LANES), np.float32)
    for h in range(n_heads):
        for part in range(3):
            pcum[part * g + h, g * h + part] = 1.0
            pcum[3 * g, g * h + 3 + part] = 1.0
            pcum[3 * g, LANES + g * h + part] = 1.0
            pcum[part * g + h, LANES + g * h + 3 + part] = -1.0
    return tuple(jnp.asarray(a, BF16) for a in (bq, b64, br, ltri, pcum))


def _rope_tables(pos):
    half = 16
    inv = ROPE_THETA ** (-jnp.arange(half, dtype=F32) / half)
    ang = pos.astype(F32)[:, None] * inv[None, :]
    c = jnp.tile(jnp.cos(ang), (1, LANES // half))
    s = jnp.tile(jnp.sin(ang), (1, LANES // half))
    piece = np.arange(LANES) // half
    live = piece < 4
    sel = lambda m: jnp.asarray(m.astype(np.float32))[None, :]
    return (c * sel(live), s * sel(-1.0 * (live & (piece % 2 == 0))), s * sel(live & (piece % 2 == 1)))


def _const_spec(shape, single=False):
    nd = len(shape)
    if single:
        return pl.BlockSpec(shape, lambda *_: (0,) * nd, pipeline_mode=pl.Buffered(1))
    return pl.BlockSpec(shape, lambda *_: (0,) * nd)


def _project(x, pos, wts, rows):
    n, d = x.shape
    cfg = wts["cfg"]
    n_pairs = cfg["heads"] // 2
    fox_width = cfg["heads"] * cfg["head_dim"]
    tables = _rope_tables(pos)
    consts = _proj_consts(rows, cfg["heads"], cfg["nope"], cfg["rope"], cfg["head_dim"])
    row = lambda w: pl.BlockSpec((rows, w), lambda i: (i, 0))
    col = lambda r: pl.BlockSpec((r, rows), lambda i: (0, i))
    weights = (wts["g_attn"], wts["w_main"], wts["g_q"], wts["w_uq"], wts["g_kv"], wts["w_uk"], wts["w_uvt"],
               wts["w_fvt"], wts["g_q256"], wts["g_kn"], wts["g_kr"], wts["g_fq"], wts["g_fk"], wts["fb"]) + consts
    in_specs = [row(d), row(LANES), row(LANES), row(LANES)] + [_const_spec(w.shape) for w in weights]
    pair_w = 2 * LANES * n_pairs
    out_shape = (
        jax.ShapeDtypeStruct((n, pair_w), BF16), jax.ShapeDtypeStruct((n, pair_w), BF16),
        jax.ShapeDtypeStruct((fox_width, n), BF16),
        jax.ShapeDtypeStruct((n, pair_w), BF16), jax.ShapeDtypeStruct((n, pair_w), BF16),
        jax.ShapeDtypeStruct((fox_width, n), BF16),
        jax.ShapeDtypeStruct((n, cfg["kv_rank"]), F32),
        jax.ShapeDtypeStruct((cfg["rope"], n), F32),
        jax.ShapeDtypeStruct((fox_width, n), F32), jax.ShapeDtypeStruct((fox_width, n), F32),
        jax.ShapeDtypeStruct((cfg["heads"], n), F32),
    )
    out_specs = (row(pair_w), row(pair_w), col(fox_width), row(pair_w), row(pair_w), col(fox_width),
                 row(cfg["kv_rank"]), col(cfg["rope"]), col(fox_width), col(fox_width), col(cfg["heads"]))
    kern = functools.partial(_proj_kernel, q_rank=cfg["q_rank"], kv_rank=cfg["kv_rank"],
                             fox_width=fox_width, n_pairs=n_pairs)
    return pl.pallas_call(
        kern, grid=(n // rows,), in_specs=in_specs, out_specs=out_specs, out_shape=out_shape,
        scratch_shapes=[pltpu.VMEM((1, LANES), F32)],
        compiler_params=pltpu.CompilerParams(dimension_semantics=("arbitrary",), vmem_limit_bytes=VMEM_LIMIT),
        name="proj",
    )(x, *tables, *weights)


def _attn_kernel(q_ref, k_ref, vt_ref, o_ref, *, tq, tk, head_dim, aug_width, aug_per_pair):
    pair = pl.program_id(0)
    qi = pl.program_id(1)
    q = q_ref[...]
    lane = lax.broadcasted_iota(jnp.int32, (1, q.shape[1]), 1)
    qm = []
    for half in range(2):
        if aug_per_pair:
            a0 = LANES + aug_width * half
        else:
            a0 = LANES + aug_width * (2 * pair + half)
        keep = ((lane >= head_dim * half) & (lane < head_dim * (half + 1))) | ((lane >= a0) & (lane < a0 + aug_width))
        qm.append(jnp.where(keep, q, jnp.zeros_like(q)))

    def step(start, size, carry, masked):
        k = k_ref[pl.ds(start, size), :]
        vt = vt_ref[:, pl.ds(start, size)]
        st = [_dot_nt(k, qm[half]) for half in range(2)]
        if masked:
            key = start + lax.broadcasted_iota(jnp.int32, (size, tq), 0)
            qry = qi * tq + lax.broadcasted_iota(jnp.int32, (size, tq), 1)
            causal = key <= qry
        out = []
        for half in range(2):
            m, l, acc = carry[half]
            s = st[half]
            if masked:
                s = jnp.where(causal, s, NEG_INF)
            m_new = jnp.maximum(m, jnp.max(s, axis=0, keepdims=True))
            alpha = jnp.exp2(m - m_new)
            p = jnp.exp2(s - m_new)
            l = alpha * l + jnp.sum(p, axis=0, keepdims=True)
            acc = alpha * acc + _dot(vt[half * head_dim:(half + 1) * head_dim, :], p.astype(BF16))
            out.append((m_new, l, acc))
        return tuple(out)

    carry = tuple((jnp.full((1, tq), NEG_INF, F32), jnp.zeros((1, tq), F32), jnp.zeros((head_dim, tq), F32))
                  for _ in range(2))
    n_big = (qi * tq) // tk
    n_small = qi - n_big * (tk // tq)
    carry = lax.fori_loop(0, n_big, lambda i, c: step(pl.multiple_of(i * tk, tk), tk, c, False), carry)
    if tk > tq:
        carry = lax.fori_loop(
            0, n_small, lambda i, c: step(pl.multiple_of(n_big * tk + i * tq, tq), tq, c, False), carry)
    carry = step(pl.multiple_of(qi * tq, tq), tq, carry, True)
    (_, l_e, acc_e), (_, l_o, acc_o) = carry
    o_t = jnp.concatenate([acc_e / l_e, acc_o / l_o], axis=0)
    o_ref[...] = o_t.T.astype(o_ref.dtype)


def _attention(q_cat, k_cat, v_t, *, head_dim, aug_width, aug_per_pair, name):
    s = q_cat.shape[0]
    n_pairs = q_cat.shape[1] // (2 * LANES)
    tq, tk = min(ATTN_TQ, s), min(ATTN_TK, s)
    kern = functools.partial(_attn_kernel, tq=tq, tk=tk, head_dim=head_dim, aug_width=aug_width,
                             aug_per_pair=aug_per_pair)
    return pl.pallas_call(
        kern, grid=(n_pairs, s // tq),
        in_specs=[pl.BlockSpec((tq, 2 * LANES), lambda j, i: (i, j)),
                  pl.BlockSpec((s, 2 * LANES), lambda j, i: (0, j)),
                  pl.BlockSpec((2 * head_dim, s), lambda j, i: (j, 0))],
        out_specs=pl.BlockSpec((tq, 2 * head_dim), lambda j, i: (i, j)),
        out_shape=jax.ShapeDtypeStruct((s, n_pairs * 2 * head_dim), BF16),
        compiler_params=pltpu.CompilerParams(dimension_semantics=("arbitrary", "arbitrary"),
                                             vmem_limit_bytes=VMEM_LIMIT),
        name=name,
    )(q_cat, k_cat, v_t)


def _sample_kernel(pt_ref, qn_ref, qr_ref, qf_ref, latn_ref, krn_ref, fkn_ref, fvn_ref, lfn_ref,
                   wuk_ref, wuv_ref, gkn_ref, seg_ref, later_ref, latern_ref, *rest,
                   n_heads, head_dim, pages_per_step):
    page_refs = rest[:5 * pages_per_step]
    oa_ref, of_ref = rest[5 * pages_per_step:5 * pages_per_step + 2]
    a_ref, qfb_ref, ma_ref, la_ref, acca_ref, mf_ref, lf_ref, accf_ref, carry_ref = rest[5 * pages_per_step + 2:]
    b = pl.program_id(0)
    step = pl.program_id(1)
    hp = a_ref.shape[0]
    width = n_heads * head_dim

    def pad_heads(x):
        return jnp.concatenate([x, jnp.zeros((hp - n_heads, x.shape[1]), F32)], axis=0).astype(BF16)

    def update(pages, later, valid):
        n = len(pages)
        w = pages[0][0].shape[0]
        lat_b = jnp.concatenate([pg[0][...].astype(BF16) for pg in pages], axis=0)
        kr_b = jnp.concatenate([pg[1][...].astype(BF16) for pg in pages], axis=1)
        fk_b = jnp.concatenate([pg[2][...].astype(BF16) for pg in pages], axis=1)
        kraw = _dot(lat_b, wuk_ref[...])
        msq = _dot_nt(seg_ref[...], (kraw * kraw).astype(BF16))[:n_heads]
        numer = _dot_nt(a_ref[...], lat_b)[:n_heads]
        rope = _dot(qr_ref[...], kr_b)[:n_heads]
        s_a = numer * lax.rsqrt(msq + EPS) + rope
        s_f = _dot(qfb_ref[...], fk_b)[:n_heads]

        parts, tots = [], []
        for pg in pages:
            lf_t = pg[4][...]
            if valid is not None:
                lf_t = jnp.where(valid, lf_t, 0.0)
            hi = lf_t.astype(BF16).astype(F32)
            parts += [hi, lf_t - hi]
            tots.append(jnp.sum(lf_t, axis=1, keepdims=True))
        suf = _dot(jnp.concatenate(parts, axis=0).astype(BF16), later)
        carry = carry_ref[...][:, 0:1]
        bias = []
        for i in range(n):
            r = 2 * n_heads * i
            bias.append(carry + suf[r:r + n_heads] + suf[r + n_heads:r + 2 * n_heads])
            carry = carry + tots[i]
        carry_ref[...] = jnp.broadcast_to(carry, carry_ref.shape)
        s_f = s_f + jnp.concatenate(bias, axis=1) * LOG2E
        if valid is not None:
            s_a = jnp.where(valid, s_a, NEG_INF)
            s_f = jnp.where(valid, s_f, NEG_INF)

        m_old = ma_ref[...][:, 0:1]
        m_new = jnp.maximum(m_old, jnp.max(s_a, axis=1, keepdims=True))
        alpha = jnp.exp2(m_old - m_new)
        p = jnp.exp2(s_a - m_new)
        la_ref[...] = alpha * la_ref[...] + jnp.sum(p, axis=1, keepdims=True)
        ma_ref[...] = jnp.broadcast_to(m_new, ma_ref.shape)
        acca_ref[...] = alpha * acca_ref[...] + _dot(pad_heads(p), lat_b)[:n_heads]

        m_old = mf_ref[...][:, 0:1]
        m_new = jnp.maximum(m_old, jnp.max(s_f, axis=1, keepdims=True))
        alpha = jnp.exp2(m_old - m_new)
        p = jnp.exp2(s_f - m_new)
        lf_ref[...] = alpha * lf_ref[...] + jnp.sum(p, axis=1, keepdims=True)
        mf_ref[...] = jnp.broadcast_to(m_new, mf_ref.shape)
        for h in range(n_heads):
            rows = slice(h * head_dim, (h + 1) * head_dim)
            acc = alpha[h:h + 1, :] * accf_ref[rows, 0:w]
            for i, pg in enumerate(pages):
                acc = acc + p[h:h + 1, i * w:(i + 1) * w] * pg[3][rows, :]
            accf_ref[rows, 0:w] = acc

    @pl.when((step == 0) & (b == 0))
    def _():
        of_ref[...] = jnp.zeros_like(of_ref)

    @pl.when(step == 0)
    def _():
        head = lax.broadcasted_iota(jnp.int32, (hp, width), 0)
        seg = lax.broadcasted_iota(jnp.int32, (hp, width), 1) // head_dim
        own = head == seg
        qg = qn_ref[0].astype(F32) * gkn_ref[...]
        q_bd = jnp.where(own, jnp.broadcast_to(qg, (hp, width)), 0.0).astype(BF16)
        a_ref[...] = _dot_nt(q_bd, wuk_ref[...]).astype(BF16)
        qfb_ref[...] = jnp.where(own, jnp.broadcast_to(qf_ref[0].astype(F32), (hp, width)), 0.0).astype(BF16)
        ma_ref[...] = jnp.full_like(ma_ref, NEG_INF)
        mf_ref[...] = jnp.full_like(mf_ref, NEG_INF)
        la_ref[...] = jnp.zeros_like(la_ref)
        lf_ref[...] = jnp.zeros_like(lf_ref)
        acca_ref[...] = jnp.zeros_like(acca_ref)
        accf_ref[...] = jnp.zeros_like(accf_ref)
        carry_ref[...] = jnp.zeros_like(carry_ref)
        nb = latn_ref.shape[0]
        valid = lax.broadcasted_iota(jnp.int32, (1, nb), 1) == b
        update([(latn_ref, krn_ref, fkn_ref, fvn_ref, lfn_ref)], latern_ref[...], valid)

    update([tuple(page_refs[5 * i:5 * i + 5]) for i in range(pages_per_step)], later_ref[...], None)

    @pl.when(step == pl.num_programs(1) - 1)
    def _():
        ctx = acca_ref[...] / la_ref[...][:, 0:1]
        o_all = _dot(pad_heads(ctx), wuv_ref[...])[:n_heads]
        head = lax.broadcasted_iota(jnp.int32, o_all.shape, 0)
        seg = lax.broadcasted_iota(jnp.int32, o_all.shape, 1) // head_dim
        oa_ref[0] = jnp.sum(jnp.where(head == seg, o_all, 0.0), axis=0, keepdims=True)
        sums = jnp.sum(accf_ref[...], axis=1, keepdims=True)
        inv_l = 1.0 / lf_ref[...][:, 0:1]
        col = jnp.concatenate([sums[h * head_dim:(h + 1) * head_dim] * inv_l[h:h + 1, :]
                               for h in range(n_heads)], axis=0)
        lane = lax.broadcasted_iota(jnp.int32, of_ref.shape, 1)
        of_ref[...] = jnp.where(lane == b, col, of_ref[...])


def _sample_attention(qn, qr, qf, lat_new, kr_new_t, fk_new_t, fv_new_t, lf_new_t, wts,
                      cache_lat, cache_kr_t, cache_fk_t, cache_fv_t, cache_lf_t, page_table):
    cfg = wts["cfg"]
    nb, n_pages = page_table.shape
    page_size = cache_lat.shape[2]
    n_heads, head_dim = cfg["heads"], cfg["head_dim"]
    width = n_heads * head_dim
    hp = 16
    pps = min(SAMPLE_PAGES_PER_STEP, n_pages)
    n_steps = n_pages // pps
    seg = np.zeros((hp, width), np.float32)
    for h in range(n_heads):
        seg[h, h * head_dim:(h + 1) * head_dim] = 1.0 / head_dim
    seg = jnp.asarray(seg, BF16)
    qr16 = jnp.concatenate([qr, jnp.zeros((nb, hp - n_heads, qr.shape[2]), qr.dtype)], axis=1)

    per_seq = lambda shape: pl.BlockSpec((None,) + shape, lambda b, s, pt: (b,) + (0,) * len(shape))

    def page_spec(shape, i):
        def idx(b, s, pt):
            return (0, pt[b * n_pages + (n_pages - 1 - (s * pps + i))]) + (0,) * len(shape)
        return pl.BlockSpec((None, None) + shape, idx)

    later = lambda n: jnp.asarray(np.tril(np.ones((n, n), np.float32), -1), BF16)
    consts = (lat_new, kr_new_t, fk_new_t, fv_new_t, lf_new_t, wts["w_uk"], wts["w_uv"], wts["g_kn"], seg,
              later(page_size), later(nb))
    in_specs = [pl.BlockSpec((1, 1, width), lambda b, s, pt: (b, 0, 0)),
                per_seq(qr16.shape[1:]),
                pl.BlockSpec((1, 1, width), lambda b, s, pt: (b, 0, 0))]
    in_specs += [_const_spec(c.shape) for c in consts]
    page_args = []
    for i in range(pps):
        for arr in (cache_lat, cache_kr_t, cache_fk_t, cache_fv_t, cache_lf_t):
            in_specs.append(page_spec(arr.shape[2:], i))
            page_args.append(arr)
    kern = functools.partial(_sample_kernel, n_heads=n_heads, head_dim=head_dim, pages_per_step=pps)
    grid_spec = pltpu.PrefetchScalarGridSpec(
        num_scalar_prefetch=1, grid=(nb, n_steps), in_specs=in_specs,
        out_specs=(pl.BlockSpec((1, 1, width), lambda b, s, pt: (b, 0, 0)),
                   pl.BlockSpec((width, nb), lambda b, s, pt: (0, 0))),
        scratch_shapes=[pltpu.VMEM((hp, cfg["kv_rank"]), BF16), pltpu.VMEM((hp, width), BF16),
                        pltpu.VMEM((n_heads, LANES), F32), pltpu.VMEM((n_heads, LANES), F32),
                        pltpu.VMEM((n_heads, cfg["kv_rank"]), F32),
                        pltpu.VMEM((n_heads, LANES), F32), pltpu.VMEM((n_heads, LANES), F32),
                        pltpu.VMEM((width, max(page_size, nb)), F32),
                        pltpu.VMEM((n_heads, LANES), F32)])
    oa, of_t = pl.pallas_call(
        kern, grid_spec=grid_spec,
        out_shape=(jax.ShapeDtypeStruct((nb, 1, width), F32), jax.ShapeDtypeStruct((width, nb), F32)),
        compiler_params=pltpu.CompilerParams(dimension_semantics=("arbitrary", "arbitrary"),
                                             vmem_limit_bytes=VMEM_LIMIT),
        name="sample_attn",
    )(page_table.reshape(-1), qn, qr16, qf, *consts, *page_args)
    return oa.reshape(nb, width), of_t.T


def _merge_kernel(x_ref, oa_ref, ob_ref, gattn_ref, wgab_ref, wpa_ref, wpb_ref, wo_ref, gffn_ref,
                  wgu_ref, wdn_ref, y_ref):
    x = x_ref[...]
    d = x.shape[1]
    h = (_rms(x) * gattn_ref[...]).astype(BF16)
    gates = _dot(h, wgab_ref[...])
    u_a = _dot(oa_ref[...].astype(BF16), wpa_ref[...])
    u_b = _dot(ob_ref[...].astype(BF16), wpb_ref[...])
    mix = jax.nn.sigmoid(gates[:, :d]) * u_a + jax.nn.sigmoid(gates[:, d:]) * u_b
    x1 = x + _dot(mix.astype(BF16), wo_ref[...])
    h2 = (_rms(x1) * gffn_ref[...]).astype(BF16)
    gu = _dot(h2, wgu_ref[...])
    d_ff = gu.shape[1] // 2
    g, u = gu[:, :d_ff], gu[:, d_ff:]
    y_ref[...] = x1 + _dot((g * jax.nn.sigmoid(g) * u).astype(BF16), wdn_ref[...])


def _merge(x, oa, ob, wts, rows):
    n, d = x.shape
    row = lambda w: pl.BlockSpec((rows, w), lambda i: (i, 0))
    weights = (wts["g_attn"], wts["w_gab"], wts["w_pa"], wts["w_pb"], wts["w_o"], wts["g_ffn"],
               wts["w_gu"], wts["w_dn"])
    return pl.pallas_call(
        _merge_kernel, grid=(n // rows,),
        in_specs=[row(d), row(oa.shape[1]), row(ob.shape[1])] + [_const_spec(w.shape, single=True) for w in weights],
        out_specs=row(d), out_shape=jax.ShapeDtypeStruct((n, d), F32),
        compiler_params=pltpu.CompilerParams(dimension_semantics=("arbitrary",), vmem_limit_bytes=VMEM_LIMIT),
        name="merge_ffn",
    )(x, oa, ob, *weights)


def _prepare(layer, cfg, norm_attn, w_in, mla_q_norm, w_uq, mla_kv_norm, w_ukv, mla_qn_nope, mla_qn_rope,
             mla_kn_nope, mla_kn_rope, fox_qn, fox_kn, fox_fb, w_pa, w_pb, w_o, norm_ffn, w_gate_up, w_down):
    d = w_in.shape[1]
    heads, nope, rope, v_dim, hd = cfg["heads"], cfg["nope"], cfg["rope"], cfg["v_dim"], cfg["head_dim"]
    q_rank, kv_rank = cfg["q_rank"], cfg["kv_rank"]
    fw = heads * hd
    sizes = (q_rank, kv_rank, rope, fw, fw, fw, heads, d, d)
    offs = np.cumsum((0,) + sizes)
    w = w_in[layer]
    part = lambda i: w[:, offs[i]:offs[i + 1]]
    zeros = lambda c: jnp.zeros((d, c), w.dtype)
    w_main = jnp.concatenate([part(0), part(1), part(3), part(4),
                              part(2), part(2), zeros(LANES - 2 * rope),
                              part(6), zeros(LANES - heads)], axis=1).astype(BF16)
    w_gab = w[:, offs[7]:offs[9]].astype(BF16)
    wq = w_uq[layer].reshape(q_rank, heads, nope + rope)
    n_pairs = heads // 2
    wq_nope = wq[:, :, :nope].reshape(q_rank, n_pairs, 2 * nope)
    wq_rope = wq[:, :, nope:].reshape(q_rank, n_pairs, 2 * rope)
    wq_pad = jnp.zeros((q_rank, n_pairs, 2 * LANES - 2 * nope - 2 * rope), w.dtype)
    w_uq_p = jnp.concatenate([wq_nope, wq_rope, wq_pad], axis=2).reshape(q_rank, n_pairs * 2 * LANES).astype(BF16)
    wkv = w_ukv[layer].reshape(kv_rank, heads, nope + v_dim)
    w_uk = wkv[:, :, :nope].reshape(kv_rank, heads * nope).astype(BF16)
    w_uv = wkv[:, :, nope:].reshape(kv_rank, heads * v_dim).astype(BF16)
    mla_scale = float(nope + rope) ** -0.5
    fox_scale = float(hd) ** -0.5
    gn, gr = mla_qn_nope[layer], mla_qn_rope[layer]
    mla_scale, fox_scale = mla_scale * LOG2E, fox_scale * LOG2E
    g_q256 = jnp.concatenate([gn, gn, gr, gr, jnp.zeros((2 * LANES - 2 * nope - 2 * rope,), F32)]) * mla_scale
    gkr = mla_kn_rope[layer]
    g_kr = jnp.concatenate([gkr, gkr, jnp.zeros((LANES - 2 * rope,), F32)])
    fb = jnp.concatenate([fox_fb[layer], jnp.zeros((LANES - heads,), F32)])
    r2 = lambda v: v.reshape(1, -1).astype(F32)
    return dict(
        cfg=cfg, g_attn=r2(norm_attn[layer]), w_main=w_main, w_gab=w_gab, g_q=r2(mla_q_norm[layer]), w_uq=w_uq_p,
        g_kv=r2(mla_kv_norm[layer]), w_uk=w_uk, w_uv=w_uv, w_uvt=w_uv.T, w_fvt=part(5).T.astype(BF16), g_q256=r2(g_q256),
        g_kn=r2(jnp.tile(mla_kn_nope[layer], heads)), g_kr=r2(g_kr),
        g_fq=r2(jnp.tile(fox_qn[layer], heads) * fox_scale), g_fk=r2(jnp.tile(fox_kn[layer], heads)), fb=r2(fb),
        w_pa=w_pa[layer].astype(BF16), w_pb=w_pb[layer].astype(BF16), w_o=w_o[layer].astype(BF16),
        g_ffn=r2(norm_ffn[layer]), w_gu=w_gate_up[layer].astype(BF16), w_dn=w_down[layer].astype(BF16))


def kernel(x_prompt, x_sample, cache_mla_latent, cache_mla_krope, cache_fox_k, cache_fox_v, cache_fox_logf, page_table, norm_attn, w_in, mla_q_norm, w_uq, mla_kv_norm, w_ukv, mla_qn_nope, mla_qn_rope, mla_kn_nope, mla_kn_rope, fox_qn, fox_kn, fox_fb, w_pa, w_pb, w_o, norm_ffn, w_gate_up, w_down):
    batch, seq, d = x_prompt.shape
    nb, dec_seq, _ = x_sample.shape
    depth, n_phys, page_size, kv_rank = cache_mla_latent.shape
    heads, hd = cache_fox_k.shape[3], cache_fox_k.shape[4]
    rope = cache_mla_krope.shape[3]
    nope = mla_qn_nope.shape[1]
    assert batch == 1 and dec_seq == 1 and depth == 1, "kernel supports one prompt sequence, one new token, one layer"
    assert w_uq.shape[2] == heads * (nope + rope) and 2 * nope == LANES and 4 * rope == LANES and hd == nope
    cfg = dict(heads=heads, head_dim=hd, nope=nope, rope=rope, v_dim=w_ukv.shape[2] // heads - nope,
               q_rank=w_uq.shape[1], kv_rank=kv_rank)
    past_len = page_table.shape[1] * page_size
    layer = 0
    wts = _prepare(layer, cfg, norm_attn, w_in, mla_q_norm, w_uq, mla_kv_norm, w_ukv, mla_qn_nope, mla_qn_rope,
                   mla_kn_nope, mla_kn_rope, fox_qn, fox_kn, fox_fb, w_pa, w_pb, w_o, norm_ffn, w_gate_up, w_down)
    fw = heads * hd
    n_pairs = heads // 2

    xp = x_prompt.reshape(seq, d)
    rows_p = min(PROJ_ROWS, seq)
    qa, ka, vat, qf, kf, vft, lat_p, krt_p, fkt_p, fvt_p, lft_p = _project(xp, jnp.arange(seq), wts, rows_p)
    oa_p = _attention(qa, ka, vat, head_dim=nope, aug_width=rope, aug_per_pair=True, name="attn_mla")
    ob_p = _attention(qf, kf, vft, head_dim=hd, aug_width=FORGET_GROUP, aug_per_pair=False, name="attn_fox")
    y_p = _merge(xp, oa_p, ob_p, wts, rows_p).reshape(batch, seq, d)

    xs = x_sample.reshape(nb, d)
    pos_s = jnp.full((nb,), past_len, jnp.int32)
    qa_s, _, _, qf_s, _, _, lat_s, krt_s, fkt_s, fvt_s, lft_s = _project(xs, pos_s, wts, nb)
    qa3 = qa_s.reshape(nb, n_pairs, 2 * LANES)
    qn = qa3[:, :, :LANES].reshape(nb, 1, fw)
    qr = qa3[:, :, LANES:LANES + 2 * rope].reshape(nb, heads, rope)
    qfn = qf_s.reshape(nb, n_pairs, 2 * LANES)[:, :, :LANES].reshape(nb, 1, fw)
    cache_kr_t = jnp.transpose(cache_mla_krope, (0, 1, 3, 2))
    cache_fk_t = jnp.transpose(cache_fox_k, (0, 1, 3, 4, 2)).reshape(depth, n_phys, fw, page_size)
    cache_fv_t = jnp.transpose(cache_fox_v, (0, 1, 3, 4, 2)).reshape(depth, n_phys, fw, page_size)
    cache_lf_t = jnp.transpose(cache_fox_logf, (0, 1, 3, 2))
    oa_s, ob_s = _sample_attention(qn, qr, qfn, lat_s, krt_s, fkt_s, fvt_s, lft_s, wts,
                                   cache_mla_latent, cache_kr_t, cache_fk_t, cache_fv_t, cache_lf_t, page_table)
    y_s = _merge(xs, oa_s, ob_s, wts, nb).reshape(nb, dec_seq, d)

    def cache_outputs(lat, krt, fkt, fvt, lft, lead):
        n = lat.shape[0]
        shape = lambda *tail: (depth,) + lead + tail
        return (lat.reshape(shape(kv_rank)),
                krt.T.reshape(shape(rope)),
                jnp.transpose(fkt.reshape(heads, hd, n), (2, 0, 1)).reshape(shape(heads, hd)),
                jnp.transpose(fvt.reshape(heads, hd, n), (2, 0, 1)).reshape(shape(heads, hd)),
                lft.T.reshape(shape(heads)))

    return (y_p, y_s) + cache_outputs(lat_p, krt_p, fkt_p, fvt_p, lft_p, (batch, seq)) \
        + cache_outputs(lat_s, krt_s, fkt_s, fvt_s, lft_s, (nb, dec_seq))
```

```python
import functools

import numpy as np
import jax
import jax.numpy as jnp
from jax import lax
from jax.experimental import pallas as pl
from jax.experimental.pallas import tpu as pltpu

F32 = jnp.float32
BF16 = jnp.bfloat16

EPS = 1e-6
NEG_INF = -1e30
ROPE_THETA = 10000.0
LOG2E = 1.4426950408889634
FORGET_GROUP = 8
LANES = 128
VMEM_LIMIT = 56 * 1024 * 1024

PROJ_ROWS = 256
ATTN_TQ = 256
ATTN_TK = 1024
SAMPLE_PAGES_PER_STEP = 8


def _dot(a, b):
    return jnp.dot(a, b, preferred_element_type=F32)


def _dot_nt(a, b):
    return lax.dot_general(a, b, (((1,), (1,)), ((), ())), preferred_element_type=F32)


def _rms(x):
    return x * lax.rsqrt(jnp.mean(x * x, axis=-1, keepdims=True) + EPS)


def _seg_norm(x, seg_mean):
    ms = _dot((x * x).astype(BF16), seg_mean)
    return x * lax.rsqrt(ms + EPS)


def _rope(t, cos_t, sin_a, sin_b):
    half = 16
    return t * cos_t + pltpu.roll(t, LANES - half, 1) * sin_a + pltpu.roll(t, half, 1) * sin_b


def _split3(x):
    hi = x.astype(BF16)
    r = x - hi.astype(F32)
    mid = r.astype(BF16)
    lo = (r - mid.astype(F32)).astype(BF16)
    return hi, mid, lo


def _proj_kernel(x_ref, cos_ref, sina_ref, sinb_ref, gattn_ref, wmain_ref, gq_ref, wuq_ref, gkv_ref,
                 wuk_ref, wuvt_ref, wfvt_ref, gq256_ref, gkn_ref, gkr_ref, gfq_ref, gfk_ref, fb_ref,
                 bq_ref, b64_ref, br_ref, ltri_ref, pcum_ref,
                 qa_ref, ka_ref, vat_ref, qf_ref, kf_ref, vft_ref,
                 lat_ref, krt_ref, fkt_ref, fvt_ref, lft_ref,
                 carry_ref, *, q_rank, kv_rank, fox_width, n_pairs):
    @pl.when(pl.program_id(0) == 0)
    def _():
        carry_ref[...] = jnp.zeros_like(carry_ref)

    cos_t, sin_a, sin_b = cos_ref[...], sina_ref[...], sinb_ref[...]
    h = (_rms(x_ref[...]) * gattn_ref[...]).astype(BF16)

    o_kv = q_rank
    o_fq = o_kv + kv_rank
    o_fk = o_fq + fox_width
    o_kr = o_fk + fox_width
    o_ff = o_kr + LANES

    c_q = _dot(h, wmain_ref[:, 0:o_kv])
    cqn = (_rms(c_q) * gq_ref[...]).astype(BF16)
    q_all = _dot(cqn, wuq_ref[...])
    for j in range(n_pairs):
        blk = q_all[:, 2 * LANES * j:2 * LANES * (j + 1)]
        blk = _seg_norm(blk, bq_ref[...]) * gq256_ref[...]
        rope = _rope(blk[:, LANES:], cos_t, sin_a, sin_b)
        qa_ref[:, 2 * LANES * j:2 * LANES * j + LANES] = blk[:, :LANES].astype(BF16)
        qa_ref[:, 2 * LANES * j + LANES:2 * LANES * (j + 1)] = rope.astype(BF16)

    c_kv = _dot(h, wmain_ref[:, o_kv:o_fq])
    latent = _rms(c_kv) * gkv_ref[...]
    lat_ref[...] = latent
    lat_b = latent.astype(BF16)
    kn = _seg_norm(_dot(lat_b, wuk_ref[...]), b64_ref[...]) * gkn_ref[...]
    vat_ref[...] = _dot_nt(wuvt_ref[...], lat_b).astype(BF16)
    kr = _dot(h, wmain_ref[:, o_kr:o_ff])
    kr = _rope(_seg_norm(kr, br_ref[...]) * gkr_ref[...], cos_t, sin_a, sin_b)
    krt_ref[...] = kr.T[0:krt_ref.shape[0], :]
    kr_b = kr.astype(BF16)
    kn_b = kn.astype(BF16)
    for j in range(n_pairs):
        ka_ref[:, 2 * LANES * j:2 * LANES * j + LANES] = kn_b[:, LANES * j:LANES * (j + 1)]
        ka_ref[:, 2 * LANES * j + LANES:2 * LANES * (j + 1)] = kr_b

    fq = _seg_norm(_dot(h, wmain_ref[:, o_fq:o_fk]), b64_ref[...]) * gfq_ref[...]
    fk = _seg_norm(_dot(h, wmain_ref[:, o_fk:o_kr]), b64_ref[...]) * gfk_ref[...]
    fkt_ref[...] = fk.T
    fv_t = _dot_nt(wfvt_ref[...], h)
    fvt_ref[...] = fv_t
    vft_ref[...] = fv_t.astype(BF16)

    ff = _dot(h, wmain_ref[:, o_ff:o_ff + LANES]) + fb_ref[...]
    logf = jnp.minimum(ff, 0.0) - jnp.log1p(jnp.exp(-jnp.abs(ff)))
    lane = lax.broadcasted_iota(jnp.int32, logf.shape, 1)
    logf = jnp.where(lane < lft_ref.shape[0], logf, 0.0)
    lft_ref[...] = logf.T[0:lft_ref.shape[0], :]
    l_hi, l_mid, l_lo = _split3(logf)
    ltri = ltri_ref[...]
    cum = carry_ref[...] + (_dot(ltri, l_hi) + _dot(ltri, l_mid) + _dot(ltri, l_lo))
    carry_ref[...] = cum[cum.shape[0] - 1:, :]
    c_hi, c_mid, c_lo = _split3(cum * LOG2E)
    packed = (c_hi.astype(F32) + pltpu.roll(c_mid.astype(F32), FORGET_GROUP, 1)
              + pltpu.roll(c_lo.astype(F32), 2 * FORGET_GROUP, 1)
              + jnp.where(lane == 3 * FORGET_GROUP, 1.0, 0.0))
    aug = _dot(packed.astype(BF16), pcum_ref[...]).astype(BF16)
    fq_b = fq.astype(BF16)
    fk_b = fk.astype(BF16)
    for j in range(n_pairs):
        qf_ref[:, 2 * LANES * j:2 * LANES * j + LANES] = fq_b[:, LANES * j:LANES * (j + 1)]
        qf_ref[:, 2 * LANES * j + LANES:2 * LANES * (j + 1)] = aug[:, :LANES]
        kf_ref[:, 2 * LANES * j:2 * LANES * j + LANES] = fk_b[:, LANES * j:LANES * (j + 1)]
        kf_ref[:, 2 * LANES * j + LANES:2 * LANES * (j + 1)] = aug[:, LANES:]


def _block_diag(sizes_scales, n):
    m = np.zeros((n, n), np.float32)
    o = 0
    for size, scale in sizes_scales:
        m[o:o + size, o:o + size] = scale
        o += size
    return m


def _proj_consts(rows, n_heads, nope, rope_dim, head_dim):
    bq = _block_diag([(nope, 1.0 / nope)] * 2 + [(rope_dim, 1.0 / rope_dim)] * 2, 2 * LANES)
    b64 = _block_diag([(head_dim, 1.0 / head_dim)] * n_heads, n_heads * head_dim)
    br = _block_diag([(rope_dim, 1.0 / rope_dim)] * 2, LANES)
    ltri = np.tril(np.ones((rows, rows), np.float32))
    g = FORGET_GROUP
    pcum = np.zeros((LANES, 2 * LANES), np.float32)
    for h in range(n_heads):
        for part in range(3):
            pcum[part * g + h, g * h + part] = 1.0
            pcum[3 * g, g * h + 3 + part] = 1.0
            pcum[3 * g, LANES + g * h + part] = 1.0
            pcum[part * g + h, LANES + g * h + 3 + part] = -1.0
    return tuple(jnp.asarray(a, BF16) for a in (bq, b64, br, ltri, pcum))


def _rope_tables(pos):
    half = 16
    inv = ROPE_THETA ** (-jnp.arange(half, dtype=F32) / half)
    ang = pos.astype(F32)[:, None] * inv[None, :]
    c = jnp.tile(jnp.cos(ang), (1, LANES // half))
    s = jnp.tile(jnp.sin(ang), (1, LANES // half))
    piece = np.arange(LANES) // half
    live = piece < 4
    sel = lambda m: jnp.asarray(m.astype(np.float32))[None, :]
    return (c * sel(live), s * sel(-1.0 * (live & (piece % 2 == 0))), s * sel(live & (piece % 2 == 1)))


def _const_spec(shape, single=False):
    nd = len(shape)
    if single:
        return pl.BlockSpec(shape, lambda *_: (0,) * nd, pipeline_mode=pl.Buffered(1))
    return pl.BlockSpec(shape, lambda *_: (0,) * nd)


def _project(x, pos, wts, rows):
    n, d = x.shape
    cfg = wts["cfg"]
    n_pairs = cfg["heads"] // 2
    fox_width = cfg["heads"] * cfg["head_dim"]
    tables = _rope_tables(pos)
    consts = _proj_consts(rows, cfg["heads"], cfg["nope"], cfg["rope"], cfg["head_dim"])
    row = lambda w: pl.BlockSpec((rows, w), lambda i: (i, 0))
    col = lambda r: pl.BlockSpec((r, rows), lambda i: (0, i))
    weights = (wts["g_attn"], wts["w_main"], wts["g_q"], wts["w_uq"], wts["g_kv"], wts["w_uk"], wts["w_uvt"],
               wts["w_fvt"], wts["g_q256"], wts["g_kn"], wts["g_kr"], wts["g_fq"], wts["g_fk"], wts["fb"]) + consts
    in_specs = [row(d), row(LANES), row(LANES), row(LANES)] + [_const_spec(w.shape) for w in weights]
    pair_w = 2 * LANES * n_pairs
    out_shape = (
        jax.ShapeDtypeStruct((n, pair_w), BF16), jax.ShapeDtypeStruct((n, pair_w), BF16),
        jax.ShapeDtypeStruct((fox_width, n), BF16),
        jax.ShapeDtypeStruct((n, pair_w), BF16), jax.ShapeDtypeStruct((n, pair_w), BF16),
        jax.ShapeDtypeStruct((fox_width, n), BF16),
        jax.ShapeDtypeStruct((n, cfg["kv_rank"]), F32),
        jax.ShapeDtypeStruct((cfg["rope"], n), F32),
        jax.ShapeDtypeStruct((fox_width, n), F32), jax.ShapeDtypeStruct((fox_width, n), F32),
        jax.ShapeDtypeStruct((cfg["heads"], n), F32),
    )
    out_specs = (row(pair_w), row(pair_w), col(fox_width), row(pair_w), row(pair_w), col(fox_width),
                 row(cfg["kv_rank"]), col(cfg["rope"]), col(fox_width), col(fox_width), col(cfg["heads"]))
    kern = functools.partial(_proj_kernel, q_rank=cfg["q_rank"], kv_rank=cfg["kv_rank"],
                             fox_width=fox_width, n_pairs=n_pairs)
    return pl.pallas_call(
        kern, grid=(n // rows,), in_specs=in_specs, out_specs=out_specs, out_shape=out_shape,
        scratch_shapes=[pltpu.VMEM((1, LANES), F32)],
        compiler_params=pltpu.CompilerParams(dimension_semantics=("arbitrary",), vmem_limit_bytes=VMEM_LIMIT),
        name="proj",
    )(x, *tables, *weights)


def _attn_kernel(q_ref, k_ref, vt_ref, o_ref, st_ref, *, tq, tk, head_dim, aug_width, aug_per_pair):
    pair = pl.program_id(0)
    qi = pl.program_id(1)
    q = q_ref[...]
    lane = lax.broadcasted_iota(jnp.int32, (1, q.shape[1]), 1)
    qm = []
    for half in range(2):
        if aug_per_pair:
            a0 = LANES + aug_width * half
        else:
            a0 = LANES + aug_width * (2 * pair + half)
        keep = ((lane >= head_dim * half) & (lane < head_dim * (half + 1))) | ((lane >= a0) & (lane < a0 + aug_width))
        qm.append(jnp.where(keep, q, jnp.zeros_like(q)))

    def scores(start, size):
        k = k_ref[pl.ds(start, size), :]
        return [_dot_nt(k, qm[half]) for half in range(2)]

    def softmax_pv(st, start, size, carry, masked):
        vt = vt_ref[:, pl.ds(start, size)]
        if masked:
            key = start + lax.broadcasted_iota(jnp.int32, (size, tq), 0)
            qry = qi * tq + lax.broadcasted_iota(jnp.int32, (size, tq), 1)
            causal = key <= qry
        out = []
        for half in range(2):
            m, l, acc = carry[half]
            s = st[half]
            if masked:
                s = jnp.where(causal, s, NEG_INF)
            m_new = jnp.maximum(m, jnp.max(s, axis=0, keepdims=True))
            alpha = jnp.exp2(m - m_new)
            p = jnp.exp2(s - m_new)
            l = alpha * l + jnp.sum(p, axis=0, keepdims=True)
            acc = alpha * acc + _dot(vt[half * head_dim:(half + 1) * head_dim, :], p.astype(BF16))
            out.append((m_new, l, acc))
        return tuple(out)

    def step(start, size, carry, masked):
        return softmax_pv(scores(start, size), start, size, carry, masked)

    def scores_to(slot, block):
        st = scores(pl.multiple_of(block * tk, tk), tk)
        for half in range(2):
            st_ref[slot, half] = st[half]

    def softmax_from(slot, block, carry):
        st = [st_ref[slot, half] for half in range(2)]
        return softmax_pv(st, pl.multiple_of(block * tk, tk), tk, carry, False)

    carry = tuple((jnp.full((1, tq), NEG_INF, F32), jnp.zeros((1, tq), F32), jnp.zeros((head_dim, tq), F32))
                  for _ in range(2))
    n_big = (qi * tq) // tk
    n_small = qi - n_big * (tk // tq)

    scores_to(0, 0)

    def pair_body(i, c):
        scores_to(1, 2 * i + 1)
        c = softmax_from(0, 2 * i, c)
        scores_to(0, 2 * i + 2)
        return softmax_from(1, 2 * i + 1, c)

    carry = lax.fori_loop(0, n_big // 2, pair_body, carry)
    carry = lax.cond(n_big % 2 == 1, lambda c: softmax_from(0, n_big - 1, c), lambda c: c, carry)
    if tk > tq:
        carry = lax.fori_loop(
            0, n_small, lambda i, c: step(pl.multiple_of(n_big * tk + i * tq, tq), tq, c, False), carry)
    carry = step(pl.multiple_of(qi * tq, tq), tq, carry, True)
    (_, l_e, acc_e), (_, l_o, acc_o) = carry
    o_t = jnp.concatenate([acc_e / l_e, acc_o / l_o], axis=0)
    o_ref[...] = o_t.T.astype(o_ref.dtype)


def _attention(q_cat, k_cat, v_t, *, head_dim, aug_width, aug_per_pair, name):
    s = q_cat.shape[0]
    n_pairs = q_cat.shape[1] // (2 * LANES)
    tq, tk = min(ATTN_TQ, s), min(ATTN_TK, s)
    assert s % tk == 0 and tk % tq == 0
    kern = functools.partial(_attn_kernel, tq=tq, tk=tk, head_dim=head_dim, aug_width=aug_width,
                             aug_per_pair=aug_per_pair)
    return pl.pallas_call(
        kern, grid=(n_pairs, s // tq),
        in_specs=[pl.BlockSpec((tq, 2 * LANES), lambda j, i: (i, j)),
                  pl.BlockSpec((s, 2 * LANES), lambda j, i: (0, j)),
                  pl.BlockSpec((2 * head_dim, s), lambda j, i: (j, 0))],
        out_specs=pl.BlockSpec((tq, 2 * head_dim), lambda j, i: (i, j)),
        out_shape=jax.ShapeDtypeStruct((s, n_pairs * 2 * head_dim), BF16),
        scratch_shapes=[pltpu.VMEM((2, 2, tk, tq), F32)],
        compiler_params=pltpu.CompilerParams(dimension_semantics=("arbitrary", "arbitrary"),
                                             vmem_limit_bytes=VMEM_LIMIT),
        name=name,
    )(q_cat, k_cat, v_t)


def _sample_kernel(pt_ref, qn_ref, qr_ref, qf_ref, latn_ref, krn_ref, fkn_ref, fvn_ref, lfn_ref,
                   wukt_ref, wuv_ref, gkn_ref, later_ref, latern_ref, *rest,
                   n_heads, head_dim, pages_per_step):
    page_refs = rest[:5 * pages_per_step]
    oa_ref, of_ref = rest[5 * pages_per_step:5 * pages_per_step + 2]
    lhs_ref, qfb_ref, ma_ref, la_ref, acca_ref, mf_ref, lf_ref, accf_ref, carry_ref = rest[5 * pages_per_step + 2:]
    b = pl.program_id(0)
    step = pl.program_id(1)
    width = n_heads * head_dim
    hp = lhs_ref.shape[0] - width

    def pad_heads(x):
        return jnp.concatenate([x, jnp.zeros((hp - n_heads, x.shape[1]), F32)], axis=0).astype(BF16)

    def update(pages, later, valid):
        n = len(pages)
        w = pages[0][0].shape[0]
        lat_b = jnp.concatenate([pg[0][...].astype(BF16) for pg in pages], axis=0)
        kr_b = jnp.concatenate([pg[1][...].astype(BF16) for pg in pages], axis=1)
        fk_b = jnp.concatenate([pg[2][...].astype(BF16) for pg in pages], axis=1)
        both = _dot_nt(lhs_ref[...], lat_b)
        numer = both[:n_heads]
        sq = both[hp:] * both[hp:]
        msq = jnp.concatenate([jnp.sum(sq[h * head_dim:(h + 1) * head_dim], axis=0, keepdims=True)
                               for h in range(n_heads)], axis=0) * (1.0 / head_dim)
        rope = _dot(qr_ref[...], kr_b)[:n_heads]
        s_a = numer * lax.rsqrt(msq + EPS) + rope
        s_f = _dot(qfb_ref[...], fk_b)[:n_heads]

        parts, tots = [], []
        for pg in pages:
            lf_t = pg[4][...]
            if valid is not None:
                lf_t = jnp.where(valid, lf_t, 0.0)
            hi = lf_t.astype(BF16).astype(F32)
            parts += [hi, lf_t - hi]
            tots.append(jnp.sum(lf_t, axis=1, keepdims=True))
        suf = _dot(jnp.concatenate(parts, axis=0).astype(BF16), later)
        carry = carry_ref[...][:, 0:1]
        bias = []
        for i in range(n):
            r = 2 * n_heads * i
            bias.append(carry + suf[r:r + n_heads] + suf[r + n_heads:r + 2 * n_heads])
            carry = carry + tots[i]
        carry_ref[...] = jnp.broadcast_to(carry, carry_ref.shape)
        s_f = s_f + jnp.concatenate(bias, axis=1) * LOG2E
        if valid is not None:
            s_a = jnp.where(valid, s_a, NEG_INF)
            s_f = jnp.where(valid, s_f, NEG_INF)

        m_old = ma_ref[...][:, 0:1]
        m_new = jnp.maximum(m_old, jnp.max(s_a, axis=1, keepdims=True))
        alpha = jnp.exp2(m_old - m_new)
        p = jnp.exp2(s_a - m_new)
        la_ref[...] = alpha * la_ref[...] + jnp.sum(p, axis=1, keepdims=True)
        ma_ref[...] = jnp.broadcast_to(m_new, ma_ref.shape)
        acca_ref[...] = alpha * acca_ref[...] + _dot(pad_heads(p), lat_b)[:n_heads]

        m_old = mf_ref[...][:, 0:1]
        m_new = jnp.maximum(m_old, jnp.max(s_f, axis=1, keepdims=True))
        alpha = jnp.exp2(m_old - m_new)
        p = jnp.exp2(s_f - m_new)
        lf_ref[...] = alpha * lf_ref[...] + jnp.sum(p, axis=1, keepdims=True)
        mf_ref[...] = jnp.broadcast_to(m_new, mf_ref.shape)
        for h in range(n_heads):
            rows = slice(h * head_dim, (h + 1) * head_dim)
            acc = alpha[h:h + 1, :] * accf_ref[rows, 0:w]
            for i, pg in enumerate(pages):
                acc = acc + p[h:h + 1, i * w:(i + 1) * w] * pg[3][rows, :]
            accf_ref[rows, 0:w] = acc

    @pl.when((step == 0) & (b == 0))
    def _():
        of_ref[...] = jnp.zeros_like(of_ref)

    @pl.when(step == 0)
    def _():
        head = lax.broadcasted_iota(jnp.int32, (hp, width), 0)
        seg = lax.broadcasted_iota(jnp.int32, (hp, width), 1) // head_dim
        own = head == seg
        qg = qn_ref[0].astype(F32) * gkn_ref[...]
        q_bd = jnp.where(own, jnp.broadcast_to(qg, (hp, width)), 0.0).astype(BF16)
        lhs_ref[0:hp, :] = _dot(q_bd, wukt_ref[...]).astype(BF16)
        lhs_ref[hp:, :] = wukt_ref[...]
        qfb_ref[...] = jnp.where(own, jnp.broadcast_to(qf_ref[0].astype(F32), (hp, width)), 0.0).astype(BF16)
        ma_ref[...] = jnp.full_like(ma_ref, NEG_INF)
        mf_ref[...] = jnp.full_like(mf_ref, NEG_INF)
        la_ref[...] = jnp.zeros_like(la_ref)
        lf_ref[...] = jnp.zeros_like(lf_ref)
        acca_ref[...] = jnp.zeros_like(acca_ref)
        accf_ref[...] = jnp.zeros_like(accf_ref)
        carry_ref[...] = jnp.zeros_like(carry_ref)
        nb = latn_ref.shape[0]
        valid = lax.broadcasted_iota(jnp.int32, (1, nb), 1) == b
        update([(latn_ref, krn_ref, fkn_ref, fvn_ref, lfn_ref)], latern_ref[...], valid)

    update([tuple(page_refs[5 * i:5 * i + 5]) for i in range(pages_per_step)], later_ref[...], None)

    @pl.when(step == pl.num_programs(1) - 1)
    def _():
        ctx = acca_ref[...] / la_ref[...][:, 0:1]
        o_all = _dot(pad_heads(ctx), wuv_ref[...])[:n_heads]
        head = lax.broadcasted_iota(jnp.int32, o_all.shape, 0)
        seg = lax.broadcasted_iota(jnp.int32, o_all.shape, 1) // head_dim
        oa_ref[0] = jnp.sum(jnp.where(head == seg, o_all, 0.0), axis=0, keepdims=True)
        sums = jnp.sum(accf_ref[...], axis=1, keepdims=True)
        inv_l = 1.0 / lf_ref[...][:, 0:1]
        col = jnp.concatenate([sums[h * head_dim:(h + 1) * head_dim] * inv_l[h:h + 1, :]
                               for h in range(n_heads)], axis=0)
        lane = lax.broadcasted_iota(jnp.int32, of_ref.shape, 1)
        of_ref[...] = jnp.where(lane == b, col, of_ref[...])


def _sample_attention(qn, qr, qf, lat_new, kr_new_t, fk_new_t, fv_new_t, lf_new_t, wts,
                      cache_lat, cache_kr_t, cache_fk_t, cache_fv_t, cache_lf_t, page_table):
    cfg = wts["cfg"]
    nb, n_pages = page_table.shape
    page_size = cache_lat.shape[2]
    n_heads, head_dim = cfg["heads"], cfg["head_dim"]
    width = n_heads * head_dim
    hp = 16
    pps = min(SAMPLE_PAGES_PER_STEP, n_pages)
    n_steps = n_pages // pps
    qr16 = jnp.concatenate([qr, jnp.zeros((nb, hp - n_heads, qr.shape[2]), qr.dtype)], axis=1)

    per_seq = lambda shape: pl.BlockSpec((None,) + shape, lambda b, s, pt: (b,) + (0,) * len(shape))

    def page_spec(shape, i):
        def idx(b, s, pt):
            return (0, pt[b * n_pages + (n_pages - 1 - (s * pps + i))]) + (0,) * len(shape)
        return pl.BlockSpec((None, None) + shape, idx)

    later = lambda n: jnp.asarray(np.tril(np.ones((n, n), np.float32), -1), BF16)
    consts = (lat_new, kr_new_t, fk_new_t, fv_new_t, lf_new_t, wts["w_uk"].T, wts["w_uv"], wts["g_kn"],
              later(page_size), later(nb))
    in_specs = [pl.BlockSpec((1, 1, width), lambda b, s, pt: (b, 0, 0)),
                per_seq(qr16.shape[1:]),
                pl.BlockSpec((1, 1, width), lambda b, s, pt: (b, 0, 0))]
    in_specs += [_const_spec(c.shape) for c in consts]
    page_args = []
    for i in range(pps):
        for arr in (cache_lat, cache_kr_t, cache_fk_t, cache_fv_t, cache_lf_t):
            in_specs.append(page_spec(arr.shape[2:], i))
            page_args.append(arr)
    kern = functools.partial(_sample_kernel, n_heads=n_heads, head_dim=head_dim, pages_per_step=pps)
    grid_spec = pltpu.PrefetchScalarGridSpec(
        num_scalar_prefetch=1, grid=(nb, n_steps), in_specs=in_specs,
        out_specs=(pl.BlockSpec((1, 1, width), lambda b, s, pt: (b, 0, 0)),
                   pl.BlockSpec((width, nb), lambda b, s, pt: (0, 0))),
        scratch_shapes=[pltpu.VMEM((hp + width, cfg["kv_rank"]), BF16), pltpu.VMEM((hp, width), BF16),
                        pltpu.VMEM((n_heads, LANES), F32), pltpu.VMEM((n_heads, LANES), F32),
                        pltpu.VMEM((n_heads, cfg["kv_rank"]), F32),
                        pltpu.VMEM((n_heads, LANES), F32), pltpu.VMEM((n_heads, LANES), F32),
                        pltpu.VMEM((width, max(page_size, nb)), F32),
                        pltpu.VMEM((n_heads, LANES), F32)])
    oa, of_t = pl.pallas_call(
        kern, grid_spec=grid_spec,
        out_shape=(jax.ShapeDtypeStruct((nb, 1, width), F32), jax.ShapeDtypeStruct((width, nb), F32)),
        compiler_params=pltpu.CompilerParams(dimension_semantics=("arbitrary", "arbitrary"),
                                             vmem_limit_bytes=VMEM_LIMIT),
        name="sample_attn",
    )(page_table.reshape(-1), qn, qr16, qf, *consts, *page_args)
    return oa.reshape(nb, width), of_t.T


def _merge_kernel(x_ref, oa_ref, ob_ref, gattn_ref, wgab_ref, wpa_ref, wpb_ref, wo_ref, gffn_ref,
                  wgu_ref, wdn_ref, y_ref):
    x = x_ref[...]
    d = x.shape[1]
    h = (_rms(x) * gattn_ref[...]).astype(BF16)
    gates = _dot(h, wgab_ref[...])
    u_a = _dot(oa_ref[...].astype(BF16), wpa_ref[...])
    u_b = _dot(ob_ref[...].astype(BF16), wpb_ref[...])
    mix = jax.nn.sigmoid(gates[:, :d]) * u_a + jax.nn.sigmoid(gates[:, d:]) * u_b
    x1 = x + _dot(mix.astype(BF16), wo_ref[...])
    h2 = (_rms(x1) * gffn_ref[...]).astype(BF16)
    gu = _dot(h2, wgu_ref[...])
    d_ff = gu.shape[1] // 2
    g, u = gu[:, :d_ff], gu[:, d_ff:]
    y_ref[...] = x1 + _dot((g * jax.nn.sigmoid(g) * u).astype(BF16), wdn_ref[...])


def _merge(x, oa, ob, wts, rows):
    n, d = x.shape
    row = lambda w: pl.BlockSpec((rows, w), lambda i: (i, 0))
    weights = (wts["g_attn"], wts["w_gab"], wts["w_pa"], wts["w_pb"], wts["w_o"], wts["g_ffn"],
               wts["w_gu"], wts["w_dn"])
    return pl.pallas_call(
        _merge_kernel, grid=(n // rows,),
        in_specs=[row(d), row(oa.shape[1]), row(ob.shape[1])] + [_const_spec(w.shape, single=True) for w in weights],
        out_specs=row(d), out_shape=jax.ShapeDtypeStruct((n, d), F32),
        compiler_params=pltpu.CompilerParams(dimension_semantics=("arbitrary",), vmem_limit_bytes=VMEM_LIMIT),
        name="merge_ffn",
    )(x, oa, ob, *weights)


def _prepare(layer, cfg, norm_attn, w_in, mla_q_norm, w_uq, mla_kv_norm, w_ukv, mla_qn_nope, mla_qn_rope,
             mla_kn_nope, mla_kn_rope, fox_qn, fox_kn, fox_fb, w_pa, w_pb, w_o, norm_ffn, w_gate_up, w_down):
    d = w_in.shape[1]
    heads, nope, rope, v_dim, hd = cfg["heads"], cfg["nope"], cfg["rope"], cfg["v_dim"], cfg["head_dim"]
    q_rank, kv_rank = cfg["q_rank"], cfg["kv_rank"]
    fw = heads * hd
    sizes = (q_rank, kv_rank, rope, fw, fw, fw, heads, d, d)
    offs = np.cumsum((0,) + sizes)
    w = w_in[layer]
    part = lambda i: w[:, offs[i]:offs[i + 1]]
    zeros = lambda c: jnp.zeros((d, c), w.dtype)
    w_main = jnp.concatenate([part(0), part(1), part(3), part(4),
                              part(2), part(2), zeros(LANES - 2 * rope),
                              part(6), zeros(LANES - heads)], axis=1).astype(BF16)
    w_gab = w[:, offs[7]:offs[9]].astype(BF16)
    wq = w_uq[layer].reshape(q_rank, heads, nope + rope)
    n_pairs = heads // 2
    wq_nope = wq[:, :, :nope].reshape(q_rank, n_pairs, 2 * nope)
    wq_rope = wq[:, :, nope:].reshape(q_rank, n_pairs, 2 * rope)
    wq_pad = jnp.zeros((q_rank, n_pairs, 2 * LANES - 2 * nope - 2 * rope), w.dtype)
    w_uq_p = jnp.concatenate([wq_nope, wq_rope, wq_pad], axis=2).reshape(q_rank, n_pairs * 2 * LANES).astype(BF16)
    wkv = w_ukv[layer].reshape(kv_rank, heads, nope + v_dim)
    w_uk = wkv[:, :, :nope].reshape(kv_rank, heads * nope).astype(BF16)
    w_uv = wkv[:, :, nope:].reshape(kv_rank, heads * v_dim).astype(BF16)
    mla_scale = float(nope + rope) ** -0.5
    fox_scale = float(hd) ** -0.5
    gn, gr = mla_qn_nope[layer], mla_qn_rope[layer]
    mla_scale, fox_scale = mla_scale * LOG2E, fox_scale * LOG2E
    g_q256 = jnp.concatenate([gn, gn, gr, gr, jnp.zeros((2 * LANES - 2 * nope - 2 * rope,), F32)]) * mla_scale
    gkr = mla_kn_rope[layer]
    g_kr = jnp.concatenate([gkr, gkr, jnp.zeros((LANES - 2 * rope,), F32)])
    fb = jnp.concatenate([fox_fb[layer], jnp.zeros((LANES - heads,), F32)])
    r2 = lambda v: v.reshape(1, -1).astype(F32)
    return dict(
        cfg=cfg, g_attn=r2(norm_attn[layer]), w_main=w_main, w_gab=w_gab, g_q=r2(mla_q_norm[layer]), w_uq=w_uq_p,
        g_kv=r2(mla_kv_norm[layer]), w_uk=w_uk, w_uv=w_uv, w_uvt=w_uv.T, w_fvt=part(5).T.astype(BF16), g_q256=r2(g_q256),
        g_kn=r2(jnp.tile(mla_kn_nope[layer], heads)), g_kr=r2(g_kr),
        g_fq=r2(jnp.tile(fox_qn[layer], heads) * fox_scale), g_fk=r2(jnp.tile(fox_kn[layer], heads)), fb=r2(fb),
        w_pa=w_pa[layer].astype(BF16), w_pb=w_pb[layer].astype(BF16), w_o=w_o[layer].astype(BF16),
        g_ffn=r2(norm_ffn[layer]), w_gu=w_gate_up[layer].astype(BF16), w_dn=w_down[layer].astype(BF16))


def kernel(x_prompt, x_sample, cache_mla_latent, cache_mla_krope, cache_fox_k, cache_fox_v, cache_fox_logf, page_table, norm_attn, w_in, mla_q_norm, w_uq, mla_kv_norm, w_ukv, mla_qn_nope, mla_qn_rope, mla_kn_nope, mla_kn_rope, fox_qn, fox_kn, fox_fb, w_pa, w_pb, w_o, norm_ffn, w_gate_up, w_down):
    batch, seq, d = x_prompt.shape
    nb, dec_seq, _ = x_sample.shape
    depth, n_phys, page_size, kv_rank = cache_mla_latent.shape
    heads, hd = cache_fox_k.shape[3], cache_fox_k.shape[4]
    rope = cache_mla_krope.shape[3]
    nope = mla_qn_nope.shape[1]
    assert batch == 1 and dec_seq == 1 and depth == 1, "kernel supports one prompt sequence, one new token, one layer"
    assert w_uq.shape[2] == heads * (nope + rope) and 2 * nope == LANES and 4 * rope == LANES and hd == nope
    cfg = dict(heads=heads, head_dim=hd, nope=nope, rope=rope, v_dim=w_ukv.shape[2] // heads - nope,
               q_rank=w_uq.shape[1], kv_rank=kv_rank)
    past_len = page_table.shape[1] * page_size
    layer = 0
    wts = _prepare(layer, cfg, norm_attn, w_in, mla_q_norm, w_uq, mla_kv_norm, w_ukv, mla_qn_nope, mla_qn_rope,
                   mla_kn_nope, mla_kn_rope, fox_qn, fox_kn, fox_fb, w_pa, w_pb, w_o, norm_ffn, w_gate_up, w_down)
    fw = heads * hd
    n_pairs = heads // 2

    xp = x_prompt.reshape(seq, d)
    rows_p = min(PROJ_ROWS, seq)
    qa, ka, vat, qf, kf, vft, lat_p, krt_p, fkt_p, fvt_p, lft_p = _project(xp, jnp.arange(seq), wts, rows_p)
    oa_p = _attention(qa, ka, vat, head_dim=nope, aug_width=rope, aug_per_pair=True, name="attn_mla")
    ob_p = _attention(qf, kf, vft, head_dim=hd, aug_width=FORGET_GROUP, aug_per_pair=False, name="attn_fox")
    y_p = _merge(xp, oa_p, ob_p, wts, rows_p).reshape(batch, seq, d)

    xs = x_sample.reshape(nb, d)
    pos_s = jnp.full((nb,), past_len, jnp.int32)
    qa_s, _, _, qf_s, _, _, lat_s, krt_s, fkt_s, fvt_s, lft_s = _project(xs, pos_s, wts, nb)
    qa3 = qa_s.reshape(nb, n_pairs, 2 * LANES)
    qn = qa3[:, :, :LANES].reshape(nb, 1, fw)
    qr = qa3[:, :, LANES:LANES + 2 * rope].reshape(nb, heads, rope)
    qfn = qf_s.reshape(nb, n_pairs, 2 * LANES)[:, :, :LANES].reshape(nb, 1, fw)
    cache_kr_t = jnp.transpose(cache_mla_krope, (0, 1, 3, 2))
    cache_fk_t = jnp.transpose(cache_fox_k, (0, 1, 3, 4, 2)).reshape(depth, n_phys, fw, page_size)
    cache_fv_t = jnp.transpose(cache_fox_v, (0, 1, 3, 4, 2)).reshape(depth, n_phys, fw, page_size)
    cache_lf_t = jnp.transpose(cache_fox_logf, (0, 1, 3, 2))
    oa_s, ob_s = _sample_attention(qn, qr, qfn, lat_s, krt_s, fkt_s, fvt_s, lft_s, wts,
                                   cache_mla_latent, cache_kr_t, cache_fk_t, cache_fv_t, cache_lf_t, page_table)
    y_s = _merge(xs, oa_s, ob_s, wts, nb).reshape(nb, dec_seq, d)

    def cache_outputs(lat, krt, fkt, fvt, lft, lead):
        n = lat.shape[0]
        shape = lambda *tail: (depth,) + lead + tail
        return (lat.reshape(shape(kv_rank)),
                krt.T.reshape(shape(rope)),
                jnp.transpose(fkt.reshape(heads, hd, n), (2, 0, 1)).reshape(shape(heads, hd)),
                jnp.transpose(fvt.reshape(heads, hd, n), (2, 0, 1)).reshape(shape(heads, hd)),
                lft.T.reshape(shape(heads)))

    return (y_p, y_s) + cache_outputs(lat_p, krt_p, fkt_p, fvt_p, lft_p, (batch, seq)) \
        + cache_outputs(lat_s, krt_s, fkt_s, fvt_s, lft_s, (nb, dec_seq))
```

```python
import functools

import numpy as np
import jax
import jax.numpy as jnp
from jax import lax
from jax.experimental import pallas as pl
from jax.experimental.pallas import tpu as pltpu

F32 = jnp.float32
BF16 = jnp.bfloat16

EPS = 1e-6
NEG_INF = -1e30
ROPE_THETA = 10000.0
LOG2E = 1.4426950408889634
FORGET_GROUP = 8
LANES = 128
VMEM_LIMIT = 56 * 1024 * 1024

PROJ_ROWS = 256
ATTN_TQ = 256
ATTN_TK = 1024
SAMPLE_PAGES_PER_STEP = 8


def _dot(a, b):
    return jnp.dot(a, b, preferred_element_type=F32)


def _dot_nt(a, b):
    return lax.dot_general(a, b, (((1,), (1,)), ((), ())), preferred_element_type=F32)


def _rms(x):
    return x * lax.rsqrt(jnp.mean(x * x, axis=-1, keepdims=True) + EPS)


def _seg_norm(x, seg_mean):
    ms = _dot((x * x).astype(BF16), seg_mean)
    return x * lax.rsqrt(ms + EPS)


def _rope(t, cos_t, sin_a, sin_b):
    half = 16
    return t * cos_t + pltpu.roll(t, LANES - half, 1) * sin_a + pltpu.roll(t, half, 1) * sin_b


def _split3(x):
    hi = x.astype(BF16)
    r = x - hi.astype(F32)
    mid = r.astype(BF16)
    lo = (r - mid.astype(F32)).astype(BF16)
    return hi, mid, lo


def _proj_kernel(x_ref, cos_ref, sina_ref, sinb_ref, gattn_ref, wmain_ref, gq_ref, wuq_ref, gkv_ref,
                 wuk_ref, wuvt_ref, wfvt_ref, gq256_ref, gkn_ref, gkr_ref, gfq_ref, gfk_ref, fb_ref,
                 bq_ref, b64_ref, br_ref, ltri_ref, pcum_ref,
                 qa_ref, ka_ref, vat_ref, qf_ref, kf_ref, vft_ref,
                 lat_ref, krt_ref, fkt_ref, fvt_ref, lft_ref,
                 carry_ref, *, q_rank, kv_rank, fox_width, n_pairs):
    @pl.when(pl.program_id(0) == 0)
    def _():
        carry_ref[...] = jnp.zeros_like(carry_ref)

    cos_t, sin_a, sin_b = cos_ref[...], sina_ref[...], sinb_ref[...]
    h = (_rms(x_ref[...]) * gattn_ref[...]).astype(BF16)

    o_kv = q_rank
    o_fq = o_kv + kv_rank
    o_fk = o_fq + fox_width
    o_kr = o_fk + fox_width
    o_ff = o_kr + LANES

    c_q = _dot(h, wmain_ref[:, 0:o_kv])
    cqn = (_rms(c_q) * gq_ref[...]).astype(BF16)
    q_all = _dot(cqn, wuq_ref[...])
    for j in range(n_pairs):
        blk = q_all[:, 2 * LANES * j:2 * LANES * (j + 1)]
        blk = _seg_norm(blk, bq_ref[...]) * gq256_ref[...]
        rope = _rope(blk[:, LANES:], cos_t, sin_a, sin_b)
        qa_ref[:, 2 * LANES * j:2 * LANES * j + LANES] = blk[:, :LANES].astype(BF16)
        qa_ref[:, 2 * LANES * j + LANES:2 * LANES * (j + 1)] = rope.astype(BF16)

    c_kv = _dot(h, wmain_ref[:, o_kv:o_fq])
    latent = _rms(c_kv) * gkv_ref[...]
    lat_ref[...] = latent
    lat_b = latent.astype(BF16)
    kn = _seg_norm(_dot(lat_b, wuk_ref[...]), b64_ref[...]) * gkn_ref[...]
    vat_ref[...] = _dot_nt(wuvt_ref[...], lat_b).astype(BF16)
    kr = _dot(h, wmain_ref[:, o_kr:o_ff])
    kr = _rope(_seg_norm(kr, br_ref[...]) * gkr_ref[...], cos_t, sin_a, sin_b)
    krt_ref[...] = kr.T[0:krt_ref.shape[0], :]
    kr_b = kr.astype(BF16)
    kn_b = kn.astype(BF16)
    for j in range(n_pairs):
        ka_ref[:, 2 * LANES * j:2 * LANES * j + LANES] = kn_b[:, LANES * j:LANES * (j + 1)]
        ka_ref[:, 2 * LANES * j + LANES:2 * LANES * (j + 1)] = kr_b

    fq = _seg_norm(_dot(h, wmain_ref[:, o_fq:o_fk]), b64_ref[...]) * gfq_ref[...]
    fk = _seg_norm(_dot(h, wmain_ref[:, o_fk:o_kr]), b64_ref[...]) * gfk_ref[...]
    fkt_ref[...] = fk.T
    fv_t = _dot_nt(wfvt_ref[...], h)
    fvt_ref[...] = fv_t
    vft_ref[...] = fv_t.astype(BF16)

    ff = _dot(h, wmain_ref[:, o_ff:o_ff + LANES]) + fb_ref[...]
    logf = jnp.minimum(ff, 0.0) - jnp.log1p(jnp.exp(-jnp.abs(ff)))
    lane = lax.broadcasted_iota(jnp.int32, logf.shape, 1)
    logf = jnp.where(lane < lft_ref.shape[0], logf, 0.0)
    lft_ref[...] = logf.T[0:lft_ref.shape[0], :]
    l_hi, l_mid, l_lo = _split3(logf)
    ltri = ltri_ref[...]
    cum = carry_ref[...] + (_dot(ltri, l_hi) + _dot(ltri, l_mid) + _dot(ltri, l_lo))
    carry_ref[...] = cum[cum.shape[0] - 1:, :]
    c_hi, c_mid, c_lo = _split3(cum * LOG2E)
    packed = (c_hi.astype(F32) + pltpu.roll(c_mid.astype(F32), FORGET_GROUP, 1)
              + pltpu.roll(c_lo.astype(F32), 2 * FORGET_GROUP, 1)
              + jnp.where(lane == 3 * FORGET_GROUP, 1.0, 0.0))
    aug = _dot(packed.astype(BF16), pcum_ref[...]).astype(BF16)
    fq_b = fq.astype(BF16)
    fk_b = fk.astype(BF16)
    for j in range(n_pairs):
        qf_ref[:, 2 * LANES * j:2 * LANES * j + LANES] = fq_b[:, LANES * j:LANES * (j + 1)]
        qf_ref[:, 2 * LANES * j + LANES:2 * LANES * (j + 1)] = aug[:, :LANES]
        kf_ref[:, 2 * LANES * j:2 * LANES * j + LANES] = fk_b[:, LANES * j:LANES * (j + 1)]
        kf_ref[:, 2 * LANES * j + LANES:2 * LANES * (j + 1)] = aug[:, LANES:]


def _block_diag(sizes_scales, n):
    m = np.zeros((n, n), np.float32)
    o = 0
    for size, scale in sizes_scales:
        m[o:o + size, o:o + size] = scale
        o += size
    return m


def _proj_consts(rows, n_heads, nope, rope_dim, head_dim):
    bq = _block_diag([(nope, 1.0 / nope)] * 2 + [(rope_dim, 1.0 / rope_dim)] * 2, 2 * LANES)
    b64 = _block_diag([(head_dim, 1.0 / head_dim)] * n_heads, n_heads * head_dim)
    br = _block_diag([(rope_dim, 1.0 / rope_dim)] * 2, LANES)
    ltri = np.tril(np.ones((rows, rows), np.float32))
    g = FORGET_GROUP
    pcum = np.zeros((LANES, 2 * LANES), np.float32)
    for h in range(n_heads):
        for part in range(3):
            pcum[part * g + h, g * h + part] = 1.0
            pcum[3 * g, g * h + 3 + part] = 1.0
            pcum[3 * g, LANES + g * h + part] = 1.0
            pcum[part * g + h, LANES + g * h + 3 + part] = -1.0
    return tuple(jnp.asarray(a, BF16) for a in (bq, b64, br, ltri, pcum))


def _rope_tables(pos):
    half = 16
    inv = ROPE_THETA ** (-jnp.arange(half, dtype=F32) / half)
    ang = pos.astype(F32)[:, None] * inv[None, :]
    c = jnp.tile(jnp.cos(ang), (1, LANES // half))
    s = jnp.tile(jnp.sin(ang), (1, LANES // half))
    piece = np.arange(LANES) // half
    live = piece < 4
    sel = lambda m: jnp.asarray(m.astype(np.float32))[None, :]
    return (c * sel(live), s * sel(-1.0 * (live & (piece % 2 == 0))), s * sel(live & (piece % 2 == 1)))


def _const_spec(shape, single=False):
    nd = len(shape)
    if single:
        return pl.BlockSpec(shape, lambda *_: (0,) * nd, pipeline_mode=pl.Buffered(1))
    return pl.BlockSpec(shape, lambda *_: (0,) * nd)


def _project(x, pos, wts, rows):
    n, d = x.shape
    cfg = wts["cfg"]
    n_pairs = cfg["heads"] // 2
    fox_width = cfg["heads"] * cfg["head_dim"]
    tables = _rope_tables(pos)
    consts = _proj_consts(rows, cfg["heads"], cfg["nope"], cfg["rope"], cfg["head_dim"])
    row = lambda w: pl.BlockSpec((rows, w), lambda i: (i, 0))
    col = lambda r: pl.BlockSpec((r, rows), lambda i: (0, i))
    weights = (wts["g_attn"], wts["w_main"], wts["g_q"], wts["w_uq"], wts["g_kv"], wts["w_uk"], wts["w_uvt"],
               wts["w_fvt"], wts["g_q256"], wts["g_kn"], wts["g_kr"], wts["g_fq"], wts["g_fk"], wts["fb"]) + consts
    in_specs = [row(d), row(LANES), row(LANES), row(LANES)] + [_const_spec(w.shape) for w in weights]
    pair_w = 2 * LANES * n_pairs
    out_shape = (
        jax.ShapeDtypeStruct((n, pair_w), BF16), jax.ShapeDtypeStruct((n, pair_w), BF16),
        jax.ShapeDtypeStruct((fox_width, n), BF16),
        jax.ShapeDtypeStruct((n, pair_w), BF16), jax.ShapeDtypeStruct((n, pair_w), BF16),
        jax.ShapeDtypeStruct((fox_width, n), BF16),
        jax.ShapeDtypeStruct((n, cfg["kv_rank"]), F32),
        jax.ShapeDtypeStruct((cfg["rope"], n), F32),
        jax.ShapeDtypeStruct((fox_width, n), F32), jax.ShapeDtypeStruct((fox_width, n), F32),
        jax.ShapeDtypeStruct((cfg["heads"], n), F32),
    )
    out_specs = (row(pair_w), row(pair_w), col(fox_width), row(pair_w), row(pair_w), col(fox_width),
                 row(cfg["kv_rank"]), col(cfg["rope"]), col(fox_width), col(fox_width), col(cfg["heads"]))
    kern = functools.partial(_proj_kernel, q_rank=cfg["q_rank"], kv_rank=cfg["kv_rank"],
                             fox_width=fox_width, n_pairs=n_pairs)
    return pl.pallas_call(
        kern, grid=(n // rows,), in_specs=in_specs, out_specs=out_specs, out_shape=out_shape,
        scratch_shapes=[pltpu.VMEM((1, LANES), F32)],
        compiler_params=pltpu.CompilerParams(dimension_semantics=("arbitrary",), vmem_limit_bytes=VMEM_LIMIT),
        name="proj",
    )(x, *tables, *weights)


def _attn_kernel(q_ref, k_ref, vt_ref, o_ref, st_ref, *, tq, tk, head_dim, aug_width, aug_per_pair):
    pair = pl.program_id(0)
    qi = pl.program_id(1)
    q = q_ref[...]
    lane = lax.broadcasted_iota(jnp.int32, (1, q.shape[1]), 1)
    qm = []
    for half in range(2):
        if aug_per_pair:
            a0 = LANES + aug_width * half
        else:
            a0 = LANES + aug_width * (2 * pair + half)
        keep = ((lane >= head_dim * half) & (lane < head_dim * (half + 1))) | ((lane >= a0) & (lane < a0 + aug_width))
        qm.append(jnp.where(keep, q, jnp.zeros_like(q)))

    def scores(start, size):
        k = k_ref[pl.ds(start, size), :]
        return [_dot_nt(k, qm[half]) for half in range(2)]

    def softmax_pv(st, start, size, carry, masked):
        vt = vt_ref[:, pl.ds(start, size)]
        if masked:
            key = start + lax.broadcasted_iota(jnp.int32, (size, tq), 0)
            qry = qi * tq + lax.broadcasted_iota(jnp.int32, (size, tq), 1)
            causal = key <= qry
        out = []
        for half in range(2):
            m, l, acc = carry[half]
            s = st[half]
            if masked:
                s = jnp.where(causal, s, NEG_INF)
            m_new = jnp.maximum(m, jnp.max(s, axis=0, keepdims=True))
            alpha = jnp.exp2(m - m_new)
            p = jnp.exp2(s - m_new)
            l = alpha * l + jnp.sum(p, axis=0, keepdims=True)
            acc = alpha * acc + _dot(vt[half * head_dim:(half + 1) * head_dim, :], p.astype(BF16))
            out.append((m_new, l, acc))
        return tuple(out)

    def step(start, size, carry, masked):
        return softmax_pv(scores(start, size), start, size, carry, masked)

    def scores_to(slot, block):
        st = scores(pl.multiple_of(block * tk, tk), tk)
        for half in range(2):
            st_ref[slot, half] = st[half]

    def softmax_from(slot, block, carry):
        st = [st_ref[slot, half] for half in range(2)]
        return softmax_pv(st, pl.multiple_of(block * tk, tk), tk, carry, False)

    carry = tuple((jnp.full((1, tq), NEG_INF, F32), jnp.zeros((1, tq), F32), jnp.zeros((head_dim, tq), F32))
                  for _ in range(2))
    n_big = (qi * tq) // tk
    n_small = qi - n_big * (tk // tq)

    scores_to(0, 0)

    def pair_body(i, c):
        scores_to(1, 2 * i + 1)
        c = softmax_from(0, 2 * i, c)
        scores_to(0, 2 * i + 2)
        return softmax_from(1, 2 * i + 1, c)

    carry = lax.fori_loop(0, n_big // 2, pair_body, carry)
    carry = lax.cond(n_big % 2 == 1, lambda c: softmax_from(0, n_big - 1, c), lambda c: c, carry)
    if tk > tq:
        carry = lax.fori_loop(
            0, n_small, lambda i, c: step(pl.multiple_of(n_big * tk + i * tq, tq), tq, c, False), carry)
    carry = step(pl.multiple_of(qi * tq, tq), tq, carry, True)
    (_, l_e, acc_e), (_, l_o, acc_o) = carry
    o_t = jnp.concatenate([acc_e / l_e, acc_o / l_o], axis=0)
    o_ref[...] = o_t.T.astype(o_ref.dtype)


def _attention(q_cat, k_cat, v_t, *, head_dim, aug_width, aug_per_pair, name):
    s = q_cat.shape[0]
    n_pairs = q_cat.shape[1] // (2 * LANES)
    tq, tk = min(ATTN_TQ, s), min(ATTN_TK, s)
    assert s % tk == 0 and tk % tq == 0
    kern = functools.partial(_attn_kernel, tq=tq, tk=tk, head_dim=head_dim, aug_width=aug_width,
                             aug_per_pair=aug_per_pair)
    return pl.pallas_call(
        kern, grid=(n_pairs, s // tq),
        in_specs=[pl.BlockSpec((tq, 2 * LANES), lambda j, i: (i, j)),
                  pl.BlockSpec((s, 2 * LANES), lambda j, i: (0, j)),
                  pl.BlockSpec((2 * head_dim, s), lambda j, i: (j, 0))],
        out_specs=pl.BlockSpec((tq, 2 * head_dim), lambda j, i: (i, j)),
        out_shape=jax.ShapeDtypeStruct((s, n_pairs * 2 * head_dim), BF16),
        scratch_shapes=[pltpu.VMEM((2, 2, tk, tq), F32)],
        compiler_params=pltpu.CompilerParams(dimension_semantics=("arbitrary", "arbitrary"),
                                             vmem_limit_bytes=VMEM_LIMIT),
        name=name,
    )(q_cat, k_cat, v_t)


def _sample_kernel(pt_ref, qn_ref, qr_ref, qf_ref, latn_ref, krn_ref, fkn_ref, fvn_ref, lfn_ref,
                   wukt_ref, wuv_ref, gkn_ref, later_ref, latern_ref,
                   clat_hbm, ckr_hbm, cfk_hbm, cfv_hbm, clf_hbm, oa_ref, of_ref,
                   lhs_ref, qfb_ref, ma_ref, la_ref, acca_ref, mf_ref, lf_ref, accf_ref, carry_ref,
                   lat_buf, kr_buf, fk_buf, fv_buf, lf_buf, sem,
                   *, n_heads, head_dim, pages_per_step, n_pages):
    b = pl.program_id(0)
    step = pl.program_id(1)
    n_steps = pl.num_programs(1)
    width = n_heads * head_dim
    hp = lhs_ref.shape[0] - width

    caches = ((clat_hbm, lat_buf), (ckr_hbm, kr_buf), (cfk_hbm, fk_buf), (cfv_hbm, fv_buf), (clf_hbm, lf_buf))

    def page_copies(seq, stp, slot):
        out = []
        for i in range(pages_per_step):
            page = pt_ref[seq * n_pages + (n_pages - 1 - (stp * pages_per_step + i))]
            out += [pltpu.make_async_copy(hbm.at[0, page], buf.at[slot, i], sem.at[slot]) for hbm, buf in caches]
        return out

    lin = b * n_steps + step
    slot = lin % 2

    @pl.when(lin == 0)
    def _():
        for c in page_copies(b, step, slot):
            c.start()

    @pl.when(lin + 1 < pl.num_programs(0) * n_steps)
    def _():
        wrap = step + 1 == n_steps
        for c in page_copies(jnp.where(wrap, b + 1, b), jnp.where(wrap, 0, step + 1), 1 - slot):
            c.start()

    for c in page_copies(b, step, slot):
        c.wait()
    page_refs = [tuple(buf.at[slot, i] for _, buf in caches) for i in range(pages_per_step)]

    def pad_heads(x):
        return jnp.concatenate([x, jnp.zeros((hp - n_heads, x.shape[1]), F32)], axis=0).astype(BF16)

    def update(pages, later, valid):
        n = len(pages)
        w = pages[0][0].shape[0]
        lat_b = jnp.concatenate([pg[0][...].astype(BF16) for pg in pages], axis=0)
        kr_b = jnp.concatenate([pg[1][...].astype(BF16) for pg in pages], axis=1)
        fk_b = jnp.concatenate([pg[2][...].astype(BF16) for pg in pages], axis=1)
        both = _dot_nt(lhs_ref[...], lat_b)
        numer = both[:n_heads]
        sq = both[hp:] * both[hp:]
        msq = jnp.concatenate([jnp.sum(sq[h * head_dim:(h + 1) * head_dim], axis=0, keepdims=True)
                               for h in range(n_heads)], axis=0) * (1.0 / head_dim)
        rope = _dot(qr_ref[...], kr_b)[:n_heads]
        s_a = numer * lax.rsqrt(msq + EPS) + rope
        s_f = _dot(qfb_ref[...], fk_b)[:n_heads]

        parts, tots = [], []
        for pg in pages:
            lf_t = pg[4][...]
            if valid is not None:
                lf_t = jnp.where(valid, lf_t, 0.0)
            hi = lf_t.astype(BF16).astype(F32)
            parts += [hi, lf_t - hi]
            tots.append(jnp.sum(lf_t, axis=1, keepdims=True))
        suf = _dot(jnp.concatenate(parts, axis=0).astype(BF16), later)
        carry = carry_ref[...][:, 0:1]
        bias = []
        for i in range(n):
            r = 2 * n_heads * i
            bias.append(carry + suf[r:r + n_heads] + suf[r + n_heads:r + 2 * n_heads])
            carry = carry + tots[i]
        carry_ref[...] = jnp.broadcast_to(carry, carry_ref.shape)
        s_f = s_f + jnp.concatenate(bias, axis=1) * LOG2E
        if valid is not None:
            s_a = jnp.where(valid, s_a, NEG_INF)
            s_f = jnp.where(valid, s_f, NEG_INF)

        m_old = ma_ref[...][:, 0:1]
        m_new = jnp.maximum(m_old, jnp.max(s_a, axis=1, keepdims=True))
        alpha = jnp.exp2(m_old - m_new)
        p = jnp.exp2(s_a - m_new)
        la_ref[...] = alpha * la_ref[...] + jnp.sum(p, axis=1, keepdims=True)
        ma_ref[...] = jnp.broadcast_to(m_new, ma_ref.shape)
        acca_ref[...] = alpha * acca_ref[...] + _dot(pad_heads(p), lat_b)[:n_heads]

        m_old = mf_ref[...][:, 0:1]
        m_new = jnp.maximum(m_old, jnp.max(s_f, axis=1, keepdims=True))
        alpha = jnp.exp2(m_old - m_new)
        p = jnp.exp2(s_f - m_new)
        lf_ref[...] = alpha * lf_ref[...] + jnp.sum(p, axis=1, keepdims=True)
        mf_ref[...] = jnp.broadcast_to(m_new, mf_ref.shape)
        for h in range(n_heads):
            rows = slice(h * head_dim, (h + 1) * head_dim)
            acc = alpha[h:h + 1, :] * accf_ref[rows, 0:w]
            for i, pg in enumerate(pages):
                acc = acc + p[h:h + 1, i * w:(i + 1) * w] * pg[3][rows, :]
            accf_ref[rows, 0:w] = acc

    @pl.when((step == 0) & (b == 0))
    def _():
        of_ref[...] = jnp.zeros_like(of_ref)

    @pl.when(step == 0)
    def _():
        head = lax.broadcasted_iota(jnp.int32, (hp, width), 0)
        seg = lax.broadcasted_iota(jnp.int32, (hp, width), 1) // head_dim
        own = head == seg
        qg = qn_ref[0].astype(F32) * gkn_ref[...]
        q_bd = jnp.where(own, jnp.broadcast_to(qg, (hp, width)), 0.0).astype(BF16)
        lhs_ref[0:hp, :] = _dot(q_bd, wukt_ref[...]).astype(BF16)
        lhs_ref[hp:, :] = wukt_ref[...]
        qfb_ref[...] = jnp.where(own, jnp.broadcast_to(qf_ref[0].astype(F32), (hp, width)), 0.0).astype(BF16)
        ma_ref[...] = jnp.full_like(ma_ref, NEG_INF)
        mf_ref[...] = jnp.full_like(mf_ref, NEG_INF)
        la_ref[...] = jnp.zeros_like(la_ref)
        lf_ref[...] = jnp.zeros_like(lf_ref)
        acca_ref[...] = jnp.zeros_like(acca_ref)
        accf_ref[...] = jnp.zeros_like(accf_ref)
        carry_ref[...] = jnp.zeros_like(carry_ref)
        nb = latn_ref.shape[0]
        valid = lax.broadcasted_iota(jnp.int32, (1, nb), 1) == b
        update([(latn_ref, krn_ref, fkn_ref, fvn_ref, lfn_ref)], latern_ref[...], valid)

    update(page_refs, later_ref[...], None)

    @pl.when(step == pl.num_programs(1) - 1)
    def _():
        ctx = acca_ref[...] / la_ref[...][:, 0:1]
        o_all = _dot(pad_heads(ctx), wuv_ref[...])[:n_heads]
        head = lax.broadcasted_iota(jnp.int32, o_all.shape, 0)
        seg = lax.broadcasted_iota(jnp.int32, o_all.shape, 1) // head_dim
        oa_ref[0] = jnp.sum(jnp.where(head == seg, o_all, 0.0), axis=0, keepdims=True)
        sums = jnp.sum(accf_ref[...], axis=1, keepdims=True)
        inv_l = 1.0 / lf_ref[...][:, 0:1]
        col = jnp.concatenate([sums[h * head_dim:(h + 1) * head_dim] * inv_l[h:h + 1, :]
                               for h in range(n_heads)], axis=0)
        lane = lax.broadcasted_iota(jnp.int32, of_ref.shape, 1)
        of_ref[...] = jnp.where(lane == b, col, of_ref[...])


def _sample_attention(qn, qr, qf, lat_new, kr_new_t, fk_new_t, fv_new_t, lf_new_t, wts,
                      cache_lat, cache_kr_t, cache_fk_t, cache_fv_t, cache_lf_t, page_table):
    cfg = wts["cfg"]
    nb, n_pages = page_table.shape
    page_size = cache_lat.shape[2]
    n_heads, head_dim = cfg["heads"], cfg["head_dim"]
    width = n_heads * head_dim
    hp = 16
    pps = min(SAMPLE_PAGES_PER_STEP, n_pages)
    n_steps = n_pages // pps
    qr16 = jnp.concatenate([qr, jnp.zeros((nb, hp - n_heads, qr.shape[2]), qr.dtype)], axis=1)

    per_seq = lambda shape: pl.BlockSpec((None,) + shape, lambda b, s, pt: (b,) + (0,) * len(shape))

    later = lambda n: jnp.asarray(np.tril(np.ones((n, n), np.float32), -1), BF16)
    consts = (lat_new, kr_new_t, fk_new_t, fv_new_t, lf_new_t, wts["w_uk"].T, wts["w_uv"], wts["g_kn"],
              later(page_size), later(nb))
    in_specs = [pl.BlockSpec((1, 1, width), lambda b, s, pt: (b, 0, 0)),
                per_seq(qr16.shape[1:]),
                pl.BlockSpec((1, 1, width), lambda b, s, pt: (b, 0, 0))]
    in_specs += [_const_spec(c.shape) for c in consts]
    caches = (cache_lat, cache_kr_t, cache_fk_t, cache_fv_t, cache_lf_t)
    in_specs += [pl.BlockSpec(memory_space=pl.ANY) for _ in caches]
    page_bufs = [pltpu.VMEM((2, pps) + c.shape[2:], c.dtype) for c in caches]
    kern = functools.partial(_sample_kernel, n_heads=n_heads, head_dim=head_dim, pages_per_step=pps,
                             n_pages=n_pages)
    grid_spec = pltpu.PrefetchScalarGridSpec(
        num_scalar_prefetch=1, grid=(nb, n_steps), in_specs=in_specs,
        out_specs=(pl.BlockSpec((1, 1, width), lambda b, s, pt: (b, 0, 0)),
                   pl.BlockSpec((width, nb), lambda b, s, pt: (0, 0))),
        scratch_shapes=[pltpu.VMEM((hp + width, cfg["kv_rank"]), BF16), pltpu.VMEM((hp, width), BF16),
                        pltpu.VMEM((n_heads, LANES), F32), pltpu.VMEM((n_heads, LANES), F32),
                        pltpu.VMEM((n_heads, cfg["kv_rank"]), F32),
                        pltpu.VMEM((n_heads, LANES), F32), pltpu.VMEM((n_heads, LANES), F32),
                        pltpu.VMEM((width, max(page_size, nb)), F32),
                        pltpu.VMEM((n_heads, LANES), F32)] + page_bufs + [pltpu.SemaphoreType.DMA((2,))])
    oa, of_t = pl.pallas_call(
        kern, grid_spec=grid_spec,
        out_shape=(jax.ShapeDtypeStruct((nb, 1, width), F32), jax.ShapeDtypeStruct((width, nb), F32)),
        compiler_params=pltpu.CompilerParams(dimension_semantics=("arbitrary", "arbitrary"),
                                             vmem_limit_bytes=VMEM_LIMIT),
        name="sample_attn",
    )(page_table.reshape(-1), qn, qr16, qf, *consts, *caches)
    return oa.reshape(nb, width), of_t.T


def _merge_kernel(x_ref, oa_ref, ob_ref, gattn_ref, wgab_ref, wpa_ref, wpb_ref, wo_ref, gffn_ref,
                  wgu_ref, wdn_ref, y_ref):
    x = x_ref[...]
    d = x.shape[1]
    h = (_rms(x) * gattn_ref[...]).astype(BF16)
    gates = _dot(h, wgab_ref[...])
    u_a = _dot(oa_ref[...].astype(BF16), wpa_ref[...])
    u_b = _dot(ob_ref[...].astype(BF16), wpb_ref[...])
    mix = jax.nn.sigmoid(gates[:, :d]) * u_a + jax.nn.sigmoid(gates[:, d:]) * u_b
    x1 = x + _dot(mix.astype(BF16), wo_ref[...])
    h2 = (_rms(x1) * gffn_ref[...]).astype(BF16)
    gu = _dot(h2, wgu_ref[...])
    d_ff = gu.shape[1] // 2
    g, u = gu[:, :d_ff], gu[:, d_ff:]
    y_ref[...] = x1 + _dot((g * jax.nn.sigmoid(g) * u).astype(BF16), wdn_ref[...])


def _merge(x, oa, ob, wts, rows):
    n, d = x.shape
    row = lambda w: pl.BlockSpec((rows, w), lambda i: (i, 0))
    weights = (wts["g_attn"], wts["w_gab"], wts["w_pa"], wts["w_pb"], wts["w_o"], wts["g_ffn"],
               wts["w_gu"], wts["w_dn"])
    return pl.pallas_call(
        _merge_kernel, grid=(n // rows,),
        in_specs=[row(d), row(oa.shape[1]), row(ob.shape[1])] + [_const_spec(w.shape, single=True) for w in weights],
        out_specs=row(d), out_shape=jax.ShapeDtypeStruct((n, d), F32),
        compiler_params=pltpu.CompilerParams(dimension_semantics=("arbitrary",), vmem_limit_bytes=VMEM_LIMIT),
        name="merge_ffn",
    )(x, oa, ob, *weights)


def _prepare(layer, cfg, norm_attn, w_in, mla_q_norm, w_uq, mla_kv_norm, w_ukv, mla_qn_nope, mla_qn_rope,
             mla_kn_nope, mla_kn_rope, fox_qn, fox_kn, fox_fb, w_pa, w_pb, w_o, norm_ffn, w_gate_up, w_down):
    d = w_in.shape[1]
    heads, nope, rope, v_dim, hd = cfg["heads"], cfg["nope"], cfg["rope"], cfg["v_dim"], cfg["head_dim"]
    q_rank, kv_rank = cfg["q_rank"], cfg["kv_rank"]
    fw = heads * hd
    sizes = (q_rank, kv_rank, rope, fw, fw, fw, heads, d, d)
    offs = np.cumsum((0,) + sizes)
    w = w_in[layer]
    part = lambda i: w[:, offs[i]:offs[i + 1]]
    zeros = lambda c: jnp.zeros((d, c), w.dtype)
    w_main = jnp.concatenate([part(0), part(1), part(3), part(4),
                              part(2), part(2), zeros(LANES - 2 * rope),
                              part(6), zeros(LANES - heads)], axis=1).astype(BF16)
    w_gab = w[:, offs[7]:offs[9]].astype(BF16)
    wq = w_uq[layer].reshape(q_rank, heads, nope + rope)
    n_pairs = heads // 2
    wq_nope = wq[:, :, :nope].reshape(q_rank, n_pairs, 2 * nope)
    wq_rope = wq[:, :, nope:].reshape(q_rank, n_pairs, 2 * rope)
    wq_pad = jnp.zeros((q_rank, n_pairs, 2 * LANES - 2 * nope - 2 * rope), w.dtype)
    w_uq_p = jnp.concatenate([wq_nope, wq_rope, wq_pad], axis=2).reshape(q_rank, n_pairs * 2 * LANES).astype(BF16)
    wkv = w_ukv[layer].reshape(kv_rank, heads, nope + v_dim)
    w_uk = wkv[:, :, :nope].reshape(kv_rank, heads * nope).astype(BF16)
    w_uv = wkv[:, :, nope:].reshape(kv_rank, heads * v_dim).astype(BF16)
    mla_scale = float(nope + rope) ** -0.5
    fox_scale = float(hd) ** -0.5
    gn, gr = mla_qn_nope[layer], mla_qn_rope[layer]
    mla_scale, fox_scale = mla_scale * LOG2E, fox_scale * LOG2E
    g_q256 = jnp.concatenate([gn, gn, gr, gr, jnp.zeros((2 * LANES - 2 * nope - 2 * rope,), F32)]) * mla_scale
    gkr = mla_kn_rope[layer]
    g_kr = jnp.concatenate([gkr, gkr, jnp.zeros((LANES - 2 * rope,), F32)])
    fb = jnp.concatenate([fox_fb[layer], jnp.zeros((LANES - heads,), F32)])
    r2 = lambda v: v.reshape(1, -1).astype(F32)
    return dict(
        cfg=cfg, g_attn=r2(norm_attn[layer]), w_main=w_main, w_gab=w_gab, g_q=r2(mla_q_norm[layer]), w_uq=w_uq_p,
        g_kv=r2(mla_kv_norm[layer]), w_uk=w_uk, w_uv=w_uv, w_uvt=w_uv.T, w_fvt=part(5).T.astype(BF16), g_q256=r2(g_q256),
        g_kn=r2(jnp.tile(mla_kn_nope[layer], heads)), g_kr=r2(g_kr),
        g_fq=r2(jnp.tile(fox_qn[layer], heads) * fox_scale), g_fk=r2(jnp.tile(fox_kn[layer], heads)), fb=r2(fb),
        w_pa=w_pa[layer].astype(BF16), w_pb=w_pb[layer].astype(BF16), w_o=w_o[layer].astype(BF16),
        g_ffn=r2(norm_ffn[layer]), w_gu=w_gate_up[layer].astype(BF16), w_dn=w_down[layer].astype(BF16))


def kernel(x_prompt, x_sample, cache_mla_latent, cache_mla_krope, cache_fox_k, cache_fox_v, cache_fox_logf, page_table, norm_attn, w_in, mla_q_norm, w_uq, mla_kv_norm, w_ukv, mla_qn_nope, mla_qn_rope, mla_kn_nope, mla_kn_rope, fox_qn, fox_kn, fox_fb, w_pa, w_pb, w_o, norm_ffn, w_gate_up, w_down):
    batch, seq, d = x_prompt.shape
    nb, dec_seq, _ = x_sample.shape
    depth, n_phys, page_size, kv_rank = cache_mla_latent.shape
    heads, hd = cache_fox_k.shape[3], cache_fox_k.shape[4]
    rope = cache_mla_krope.shape[3]
    nope = mla_qn_nope.shape[1]
    assert batch == 1 and dec_seq == 1 and depth == 1, "kernel supports one prompt sequence, one new token, one layer"
    assert w_uq.shape[2] == heads * (nope + rope) and 2 * nope == LANES and 4 * rope == LANES and hd == nope
    cfg = dict(heads=heads, head_dim=hd, nope=nope, rope=rope, v_dim=w_ukv.shape[2] // heads - nope,
               q_rank=w_uq.shape[1], kv_rank=kv_rank)
    past_len = page_table.shape[1] * page_size
    layer = 0
    wts = _prepare(layer, cfg, norm_attn, w_in, mla_q_norm, w_uq, mla_kv_norm, w_ukv, mla_qn_nope, mla_qn_rope,
                   mla_kn_nope, mla_kn_rope, fox_qn, fox_kn, fox_fb, w_pa, w_pb, w_o, norm_ffn, w_gate_up, w_down)
    fw = heads * hd
    n_pairs = heads // 2

    xp = x_prompt.reshape(seq, d)
    rows_p = min(PROJ_ROWS, seq)
    qa, ka, vat, qf, kf, vft, lat_p, krt_p, fkt_p, fvt_p, lft_p = _project(xp, jnp.arange(seq), wts, rows_p)
    oa_p = _attention(qa, ka, vat, head_dim=nope, aug_width=rope, aug_per_pair=True, name="attn_mla")
    ob_p = _attention(qf, kf, vft, head_dim=hd, aug_width=FORGET_GROUP, aug_per_pair=False, name="attn_fox")
    y_p = _merge(xp, oa_p, ob_p, wts, rows_p).reshape(batch, seq, d)

    xs = x_sample.reshape(nb, d)
    pos_s = jnp.full((nb,), past_len, jnp.int32)
    qa_s, _, _, qf_s, _, _, lat_s, krt_s, fkt_s, fvt_s, lft_s = _project(xs, pos_s, wts, nb)
    qa3 = qa_s.reshape(nb, n_pairs, 2 * LANES)
    qn = qa3[:, :, :LANES].reshape(nb, 1, fw)
    qr = qa3[:, :, LANES:LANES + 2 * rope].reshape(nb, heads, rope)
    qfn = qf_s.reshape(nb, n_pairs, 2 * LANES)[:, :, :LANES].reshape(nb, 1, fw)
    cache_kr_t = jnp.transpose(cache_mla_krope, (0, 1, 3, 2))
    cache_fk_t = jnp.transpose(cache_fox_k, (0, 1, 3, 4, 2)).reshape(depth, n_phys, fw, page_size)
    cache_fv_t = jnp.transpose(cache_fox_v, (0, 1, 3, 4, 2)).reshape(depth, n_phys, fw, page_size)
    cache_lf_t = jnp.transpose(cache_fox_logf, (0, 1, 3, 2))
    oa_s, ob_s = _sample_attention(qn, qr, qfn, lat_s, krt_s, fkt_s, fvt_s, lft_s, wts,
                                   cache_mla_latent, cache_kr_t, cache_fk_t, cache_fv_t, cache_lf_t, page_table)
    y_s = _merge(xs, oa_s, ob_s, wts, nb).reshape(nb, dec_seq, d)

    def cache_outputs(lat, krt, fkt, fvt, lft, lead):
        n = lat.shape[0]
        shape = lambda *tail: (depth,) + lead + tail
        return (lat.reshape(shape(kv_rank)),
                krt.T.reshape(shape(rope)),
                jnp.transpose(fkt.reshape(heads, hd, n), (2, 0, 1)).reshape(shape(heads, hd)),
                jnp.transpose(fvt.reshape(heads, hd, n), (2, 0, 1)).reshape(shape(heads, hd)),
                lft.T.reshape(shape(heads)))

    return (y_p, y_s) + cache_outputs(lat_p, krt_p, fkt_p, fvt_p, lft_p, (batch, seq)) \
        + cache_outputs(lat_s, krt_s, fkt_s, fvt_s, lft_s, (nb, dec_seq))
```

```python
import functools

import numpy as np
import jax
import jax.numpy as jnp
from jax import lax
from jax.experimental import pallas as pl
from jax.experimental.pallas import tpu as pltpu

F32 = jnp.float32
BF16 = jnp.bfloat16

EPS = 1e-6
NEG_INF = -1e30
ROPE_THETA = 10000.0
LOG2E = 1.4426950408889634
FORGET_GROUP = 8
LANES = 128
VMEM_LIMIT = 56 * 1024 * 1024

PROJ_ROWS = 256
ATTN_TQ = 256
ATTN_TK = 1024
SAMPLE_PAGES_PER_STEP = 8
SAMPLE_BUFFER_SLOTS = 3


def _dot(a, b):
    return jnp.dot(a, b, preferred_element_type=F32)


def _dot_nt(a, b):
    return lax.dot_general(a, b, (((1,), (1,)), ((), ())), preferred_element_type=F32)


def _rms(x):
    return x * lax.rsqrt(jnp.mean(x * x, axis=-1, keepdims=True) + EPS)


def _seg_norm(x, seg_mean):
    ms = _dot((x * x).astype(BF16), seg_mean)
    return x * lax.rsqrt(ms + EPS)


def _rope(t, cos_t, sin_a, sin_b):
    half = 16
    return t * cos_t + pltpu.roll(t, LANES - half, 1) * sin_a + pltpu.roll(t, half, 1) * sin_b


def _split3(x):
    hi = x.astype(BF16)
    r = x - hi.astype(F32)
    mid = r.astype(BF16)
    lo = (r - mid.astype(F32)).astype(BF16)
    return hi, mid, lo


def _proj_kernel(x_ref, cos_ref, sina_ref, sinb_ref, gattn_ref, wmain_ref, gq_ref, wuq_ref, gkv_ref,
                 wuk_ref, wuvt_ref, wfvt_ref, gq256_ref, gkn_ref, gkr_ref, gfq_ref, gfk_ref, fb_ref,
                 bq_ref, b64_ref, br_ref, ltri_ref, pcum_ref,
                 qa_ref, ka_ref, vat_ref, qf_ref, kf_ref, vft_ref,
                 lat_ref, krt_ref, fkt_ref, fvt_ref, lft_ref,
                 carry_ref, *, q_rank, kv_rank, fox_width, n_pairs):
    @pl.when(pl.program_id(0) == 0)
    def _():
        carry_ref[...] = jnp.zeros_like(carry_ref)

    cos_t, sin_a, sin_b = cos_ref[...], sina_ref[...], sinb_ref[...]
    h = (_rms(x_ref[...]) * gattn_ref[...]).astype(BF16)

    o_kv = q_rank
    o_fq = o_kv + kv_rank
    o_fk = o_fq + fox_width
    o_kr = o_fk + fox_width
    o_ff = o_kr + LANES

    c_q = _dot(h, wmain_ref[:, 0:o_kv])
    cqn = (_rms(c_q) * gq_ref[...]).astype(BF16)
    q_all = _dot(cqn, wuq_ref[...])
    for j in range(n_pairs):
        blk = q_all[:, 2 * LANES * j:2 * LANES * (j + 1)]
        blk = _seg_norm(blk, bq_ref[...]) * gq256_ref[...]
        rope = _rope(blk[:, LANES:], cos_t, sin_a, sin_b)
        qa_ref[:, 2 * LANES * j:2 * LANES * j + LANES] = blk[:, :LANES].astype(BF16)
        qa_ref[:, 2 * LANES * j + LANES:2 * LANES * (j + 1)] = rope.astype(BF16)

    c_kv = _dot(h, wmain_ref[:, o_kv:o_fq])
    latent = _rms(c_kv) * gkv_ref[...]
    lat_ref[...] = latent
    lat_b = latent.astype(BF16)
    kn = _seg_norm(_dot(lat_b, wuk_ref[...]), b64_ref[...]) * gkn_ref[...]
    vat_ref[...] = _dot_nt(wuvt_ref[...], lat_b).astype(BF16)
    kr = _dot(h, wmain_ref[:, o_kr:o_ff])
    kr = _rope(_seg_norm(kr, br_ref[...]) * gkr_ref[...], cos_t, sin_a, sin_b)
    krt_ref[...] = kr.T[0:krt_ref.shape[0], :]
    kr_b = kr.astype(BF16)
    kn_b = kn.astype(BF16)
    for j in range(n_pairs):
        ka_ref[:, 2 * LANES * j:2 * LANES * j + LANES] = kn_b[:, LANES * j:LANES * (j + 1)]
        ka_ref[:, 2 * LANES * j + LANES:2 * LANES * (j + 1)] = kr_b

    fq = _seg_norm(_dot(h, wmain_ref[:, o_fq:o_fk]), b64_ref[...]) * gfq_ref[...]
    fk = _seg_norm(_dot(h, wmain_ref[:, o_fk:o_kr]), b64_ref[...]) * gfk_ref[...]
    fkt_ref[...] = fk.T
    fv_t = _dot_nt(wfvt_ref[...], h)
    fvt_ref[...] = fv_t
    vft_ref[...] = fv_t.astype(BF16)

    ff = _dot(h, wmain_ref[:, o_ff:o_ff + LANES]) + fb_ref[...]
    logf = jnp.minimum(ff, 0.0) - jnp.log1p(jnp.exp(-jnp.abs(ff)))
    lane = lax.broadcasted_iota(jnp.int32, logf.shape, 1)
    logf = jnp.where(lane < lft_ref.shape[0], logf, 0.0)
    lft_ref[...] = logf.T[0:lft_ref.shape[0], :]
    l_hi, l_mid, l_lo = _split3(logf)
    ltri = ltri_ref[...]
    cum = carry_ref[...] + (_dot(ltri, l_hi) + _dot(ltri, l_mid) + _dot(ltri, l_lo))
    carry_ref[...] = cum[cum.shape[0] - 1:, :]
    c_hi, c_mid, c_lo = _split3(cum * LOG2E)
    packed = (c_hi.astype(F32) + pltpu.roll(c_mid.astype(F32), FORGET_GROUP, 1)
              + pltpu.roll(c_lo.astype(F32), 2 * FORGET_GROUP, 1)
              + jnp.where(lane == 3 * FORGET_GROUP, 1.0, 0.0))
    aug = _dot(packed.astype(BF16), pcum_ref[...]).astype(BF16)
    fq_b = fq.astype(BF16)
    fk_b = fk.astype(BF16)
    for j in range(n_pairs):
        qf_ref[:, 2 * LANES * j:2 * LANES * j + LANES] = fq_b[:, LANES * j:LANES * (j + 1)]
        qf_ref[:, 2 * LANES * j + LANES:2 * LANES * (j + 1)] = aug[:, :LANES]
        kf_ref[:, 2 * LANES * j:2 * LANES * j + LANES] = fk_b[:, LANES * j:LANES * (j + 1)]
        kf_ref[:, 2 * LANES * j + LANES:2 * LANES * (j + 1)] = aug[:, LANES:]


def _block_diag(sizes_scales, n):
    m = np.zeros((n, n), np.float32)
    o = 0
    for size, scale in sizes_scales:
        m[o:o + size, o:o + size] = scale
        o += size
    return m


def _proj_consts(rows, n_heads, nope, rope_dim, head_dim):
    bq = _block_diag([(nope, 1.0 / nope)] * 2 + [(rope_dim, 1.0 / rope_dim)] * 2, 2 * LANES)
    b64 = _block_diag([(head_dim, 1.0 / head_dim)] * n_heads, n_heads * head_dim)
    br = _block_diag([(rope_dim, 1.0 / rope_dim)] * 2, LANES)
    ltri = np.tril(np.ones((rows, rows), np.float32))
    g = FORGET_GROUP
    pcum = np.zeros((LANES, 2 * LANES), np.float32)
    for h in range(n_heads):
        for part in range(3):
            pcum[part * g + h, g * h + part] = 1.0
            pcum[3 * g, g * h + 3 + part] = 1.0
            pcum[3 * g, LANES + g * h + part] = 1.0
            pcum[part * g + h, LANES + g * h + 3 + part] = -1.0
    return tuple(jnp.asarray(a, BF16) for a in (bq, b64, br, ltri, pcum))


def _rope_tables(pos):
    half = 16
    inv = ROPE_THETA ** (-jnp.arange(half, dtype=F32) / half)
    ang = pos.astype(F32)[:, None] * inv[None, :]
    c = jnp.tile(jnp.cos(ang), (1, LANES // half))
    s = jnp.tile(jnp.sin(ang), (1, LANES // half))
    piece = np.arange(LANES) // half
    live = piece < 4
    sel = lambda m: jnp.asarray(m.astype(np.float32))[None, :]
    return (c * sel(live), s * sel(-1.0 * (live & (piece % 2 == 0))), s * sel(live & (piece % 2 == 1)))


def _const_spec(shape, single=False):
    nd = len(shape)
    if single:
        return pl.BlockSpec(shape, lambda *_: (0,) * nd, pipeline_mode=pl.Buffered(1))
    return pl.BlockSpec(shape, lambda *_: (0,) * nd)


def _project(x, pos, wts, rows):
    n, d = x.shape
    cfg = wts["cfg"]
    n_pairs = cfg["heads"] // 2
    fox_width = cfg["heads"] * cfg["head_dim"]
    tables = _rope_tables(pos)
    consts = _proj_consts(rows, cfg["heads"], cfg["nope"], cfg["rope"], cfg["head_dim"])
    row = lambda w: pl.BlockSpec((rows, w), lambda i: (i, 0))
    col = lambda r: pl.BlockSpec((r, rows), lambda i: (0, i))
    weights = (wts["g_attn"], wts["w_main"], wts["g_q"], wts["w_uq"], wts["g_kv"], wts["w_uk"], wts["w_uvt"],
               wts["w_fvt"], wts["g_q256"], wts["g_kn"], wts["g_kr"], wts["g_fq"], wts["g_fk"], wts["fb"]) + consts
    in_specs = [row(d), row(LANES), row(LANES), row(LANES)] + [_const_spec(w.shape) for w in weights]
    pair_w = 2 * LANES * n_pairs
    out_shape = (
        jax.ShapeDtypeStruct((n, pair_w), BF16), jax.ShapeDtypeStruct((n, pair_w), BF16),
        jax.ShapeDtypeStruct((fox_width, n), BF16),
        jax.ShapeDtypeStruct((n, pair_w), BF16), jax.ShapeDtypeStruct((n, pair_w), BF16),
        jax.ShapeDtypeStruct((fox_width, n), BF16),
        jax.ShapeDtypeStruct((n, cfg["kv_rank"]), F32),
        jax.ShapeDtypeStruct((cfg["rope"], n), F32),
        jax.ShapeDtypeStruct((fox_width, n), F32), jax.ShapeDtypeStruct((fox_width, n), F32),
        jax.ShapeDtypeStruct((cfg["heads"], n), F32),
    )
    out_specs = (row(pair_w), row(pair_w), col(fox_width), row(pair_w), row(pair_w), col(fox_width),
                 row(cfg["kv_rank"]), col(cfg["rope"]), col(fox_width), col(fox_width), col(cfg["heads"]))
    kern = functools.partial(_proj_kernel, q_rank=cfg["q_rank"], kv_rank=cfg["kv_rank"],
                             fox_width=fox_width, n_pairs=n_pairs)
    return pl.pallas_call(
        kern, grid=(n // rows,), in_specs=in_specs, out_specs=out_specs, out_shape=out_shape,
        scratch_shapes=[pltpu.VMEM((1, LANES), F32)],
        compiler_params=pltpu.CompilerParams(dimension_semantics=("arbitrary",), vmem_limit_bytes=VMEM_LIMIT),
        name="proj",
    )(x, *tables, *weights)


def _attn_kernel(q_ref, k_ref, vt_ref, o_ref, st_ref, *, tq, tk, head_dim, aug_width, aug_per_pair):
    pair = pl.program_id(0)
    qi = pl.program_id(1)
    q = q_ref[...]
    lane = lax.broadcasted_iota(jnp.int32, (1, q.shape[1]), 1)
    qm = []
    for half in range(2):
        if aug_per_pair:
            a0 = LANES + aug_width * half
        else:
            a0 = LANES + aug_width * (2 * pair + half)
        keep = ((lane >= head_dim * half) & (lane < head_dim * (half + 1))) | ((lane >= a0) & (lane < a0 + aug_width))
        qm.append(jnp.where(keep, q, jnp.zeros_like(q)))

    def scores(start, size):
        k = k_ref[pl.ds(start, size), :]
        return [_dot_nt(k, qm[half]) for half in range(2)]

    def softmax_pv(st, start, size, carry, masked):
        vt = vt_ref[:, pl.ds(start, size)]
        if masked:
            key = start + lax.broadcasted_iota(jnp.int32, (size, tq), 0)
            qry = qi * tq + lax.broadcasted_iota(jnp.int32, (size, tq), 1)
            causal = key <= qry
        out = []
        for half in range(2):
            m, l, acc = carry[half]
            s = st[half]
            if masked:
                s = jnp.where(causal, s, NEG_INF)
            m_new = jnp.maximum(m, jnp.max(s, axis=0, keepdims=True))
            alpha = jnp.exp2(m - m_new)
            p = jnp.exp2(s - m_new)
            l = alpha * l + jnp.sum(p, axis=0, keepdims=True)
            acc = alpha * acc + _dot(vt[half * head_dim:(half + 1) * head_dim, :], p.astype(BF16))
            out.append((m_new, l, acc))
        return tuple(out)

    def step(start, size, carry, masked):
        return softmax_pv(scores(start, size), start, size, carry, masked)

    def scores_to(slot, block):
        st = scores(pl.multiple_of(block * tk, tk), tk)
        for half in range(2):
            st_ref[slot, half] = st[half]

    def softmax_from(slot, block, carry):
        st = [st_ref[slot, half] for half in range(2)]
        return softmax_pv(st, pl.multiple_of(block * tk, tk), tk, carry, False)

    carry = tuple((jnp.full((1, tq), NEG_INF, F32), jnp.zeros((1, tq), F32), jnp.zeros((head_dim, tq), F32))
                  for _ in range(2))
    n_big = (qi * tq) // tk
    n_small = qi - n_big * (tk // tq)

    scores_to(0, 0)

    def pair_body(i, c):
        scores_to(1, 2 * i + 1)
        c = softmax_from(0, 2 * i, c)
        scores_to(0, 2 * i + 2)
        return softmax_from(1, 2 * i + 1, c)

    carry = lax.fori_loop(0, n_big // 2, pair_body, carry)
    carry = lax.cond(n_big % 2 == 1, lambda c: softmax_from(0, n_big - 1, c), lambda c: c, carry)
    if tk > tq:
        carry = lax.fori_loop(
            0, n_small, lambda i, c: step(pl.multiple_of(n_big * tk + i * tq, tq), tq, c, False), carry)
    carry = step(pl.multiple_of(qi * tq, tq), tq, carry, True)
    (_, l_e, acc_e), (_, l_o, acc_o) = carry
    o_t = jnp.concatenate([acc_e / l_e, acc_o / l_o], axis=0)
    o_ref[...] = o_t.T.astype(o_ref.dtype)


def _attention(q_cat, k_cat, v_t, *, head_dim, aug_width, aug_per_pair, name):
    s = q_cat.shape[0]
    n_pairs = q_cat.shape[1] // (2 * LANES)
    tq, tk = min(ATTN_TQ, s), min(ATTN_TK, s)
    assert s % tk == 0 and tk % tq == 0
    kern = functools.partial(_attn_kernel, tq=tq, tk=tk, head_dim=head_dim, aug_width=aug_width,
                             aug_per_pair=aug_per_pair)
    return pl.pallas_call(
        kern, grid=(n_pairs, s // tq),
        in_specs=[pl.BlockSpec((tq, 2 * LANES), lambda j, i: (i, j)),
                  pl.BlockSpec((s, 2 * LANES), lambda j, i: (0, j)),
                  pl.BlockSpec((2 * head_dim, s), lambda j, i: (j, 0))],
        out_specs=pl.BlockSpec((tq, 2 * head_dim), lambda j, i: (i, j)),
        out_shape=jax.ShapeDtypeStruct((s, n_pairs * 2 * head_dim), BF16),
        scratch_shapes=[pltpu.VMEM((2, 2, tk, tq), F32)],
        compiler_params=pltpu.CompilerParams(dimension_semantics=("arbitrary", "arbitrary"),
                                             vmem_limit_bytes=VMEM_LIMIT),
        name=name,
    )(q_cat, k_cat, v_t)


def _sample_kernel(pt_ref, qn_ref, qr_ref, qf_ref, latn_ref, krn_ref, fkn_ref, fvn_ref, lfn_ref,
                   wukt_ref, wuv_ref, gkn_ref, later_ref, latern_ref,
                   clat_hbm, ckr_hbm, cfk_hbm, cfv_hbm, clf_hbm, oa_ref, of_ref,
                   lhs_ref, qfb_ref, ma_ref, la_ref, acca_ref, mf_ref, lf_ref, accf_ref, carry_ref,
                   lat_buf, kr_buf, fk_buf, fv_buf, lf_buf, sem,
                   *, n_heads, head_dim, pages_per_step, n_pages):
    b = pl.program_id(0)
    step = pl.program_id(1)
    n_steps = pl.num_programs(1)
    width = n_heads * head_dim
    hp = lhs_ref.shape[0] - width

    caches = ((clat_hbm, lat_buf), (ckr_hbm, kr_buf), (cfk_hbm, fk_buf), (cfv_hbm, fv_buf), (clf_hbm, lf_buf))
    n_slots = lat_buf.shape[0]
    ahead = n_slots - 1

    def page_copies(g):
        seq, stp, slot = g // n_steps, g % n_steps, g % n_slots
        out = []
        for i in range(pages_per_step):
            page = pt_ref[seq * n_pages + (n_pages - 1 - (stp * pages_per_step + i))]
            out += [pltpu.make_async_copy(hbm.at[0, page], buf.at[slot, i], sem.at[slot]) for hbm, buf in caches]
        return out

    lin = b * n_steps + step
    slot = lin % n_slots
    total = pl.num_programs(0) * n_steps

    @pl.when(lin == 0)
    def _():
        for g in range(ahead):
            @pl.when(g < total)
            def _():
                for c in page_copies(g):
                    c.start()

    @pl.when(lin + ahead < total)
    def _():
        for c in page_copies(lin + ahead):
            c.start()

    for c in page_copies(lin):
        c.wait()
    page_refs = [tuple(buf.at[slot, i] for _, buf in caches) for i in range(pages_per_step)]

    def pad_heads(x):
        return jnp.concatenate([x, jnp.zeros((hp - n_heads, x.shape[1]), F32)], axis=0).astype(BF16)

    def update(pages, later, valid):
        n = len(pages)
        w = pages[0][0].shape[0]
        lat_b = jnp.concatenate([pg[0][...].astype(BF16) for pg in pages], axis=0)
        kr_b = jnp.concatenate([pg[1][...].astype(BF16) for pg in pages], axis=1)
        fk_b = jnp.concatenate([pg[2][...].astype(BF16) for pg in pages], axis=1)
        both = _dot_nt(lhs_ref[...], lat_b)
        numer = both[:n_heads]
        sq = both[hp:] * both[hp:]
        msq = jnp.concatenate([jnp.sum(sq[h * head_dim:(h + 1) * head_dim], axis=0, keepdims=True)
                               for h in range(n_heads)], axis=0) * (1.0 / head_dim)
        rope = _dot(qr_ref[...], kr_b)[:n_heads]
        s_a = numer * lax.rsqrt(msq + EPS) + rope
        s_f = _dot(qfb_ref[...], fk_b)[:n_heads]

        parts, tots = [], []
        for pg in pages:
            lf_t = pg[4][...]
            if valid is not None:
                lf_t = jnp.where(valid, lf_t, 0.0)
            hi = lf_t.astype(BF16).astype(F32)
            parts += [hi, lf_t - hi]
            tots.append(jnp.sum(lf_t, axis=1, keepdims=True))
        suf = _dot(jnp.concatenate(parts, axis=0).astype(BF16), later)
        carry = carry_ref[...][:, 0:1]
        bias = []
        for i in range(n):
            r = 2 * n_heads * i
            bias.append(carry + suf[r:r + n_heads] + suf[r + n_heads:r + 2 * n_heads])
            carry = carry + tots[i]
        carry_ref[...] = jnp.broadcast_to(carry, carry_ref.shape)
        s_f = s_f + jnp.concatenate(bias, axis=1) * LOG2E
        if valid is not None:
            s_a = jnp.where(valid, s_a, NEG_INF)
            s_f = jnp.where(valid, s_f, NEG_INF)

        m_old = ma_ref[...][:, 0:1]
        m_new = jnp.maximum(m_old, jnp.max(s_a, axis=1, keepdims=True))
        alpha = jnp.exp2(m_old - m_new)
        p = jnp.exp2(s_a - m_new)
        la_ref[...] = alpha * la_ref[...] + jnp.sum(p, axis=1, keepdims=True)
        ma_ref[...] = jnp.broadcast_to(m_new, ma_ref.shape)
        acca_ref[...] = alpha * acca_ref[...] + _dot(pad_heads(p), lat_b)[:n_heads]

        m_old = mf_ref[...][:, 0:1]
        m_new = jnp.maximum(m_old, jnp.max(s_f, axis=1, keepdims=True))
        alpha = jnp.exp2(m_old - m_new)
        p = jnp.exp2(s_f - m_new)
        lf_ref[...] = alpha * lf_ref[...] + jnp.sum(p, axis=1, keepdims=True)
        mf_ref[...] = jnp.broadcast_to(m_new, mf_ref.shape)
        for h in range(n_heads):
            rows = slice(h * head_dim, (h + 1) * head_dim)
            acc = alpha[h:h + 1, :] * accf_ref[rows, 0:w]
            for i, pg in enumerate(pages):
                acc = acc + p[h:h + 1, i * w:(i + 1) * w] * pg[3][rows, :]
            accf_ref[rows, 0:w] = acc

    @pl.when((step == 0) & (b == 0))
    def _():
        of_ref[...] = jnp.zeros_like(of_ref)

    @pl.when(step == 0)
    def _():
        head = lax.broadcasted_iota(jnp.int32, (hp, width), 0)
        seg = lax.broadcasted_iota(jnp.int32, (hp, width), 1) // head_dim
        own = head == seg
        qg = qn_ref[0].astype(F32) * gkn_ref[...]
        q_bd = jnp.where(own, jnp.broadcast_to(qg, (hp, width)), 0.0).astype(BF16)
        lhs_ref[0:hp, :] = _dot(q_bd, wukt_ref[...]).astype(BF16)
        lhs_ref[hp:, :] = wukt_ref[...]
        qfb_ref[...] = jnp.where(own, jnp.broadcast_to(qf_ref[0].astype(F32), (hp, width)), 0.0).astype(BF16)
        ma_ref[...] = jnp.full_like(ma_ref, NEG_INF)
        mf_ref[...] = jnp.full_like(mf_ref, NEG_INF)
        la_ref[...] = jnp.zeros_like(la_ref)
        lf_ref[...] = jnp.zeros_like(lf_ref)
        acca_ref[...] = jnp.zeros_like(acca_ref)
        accf_ref[...] = jnp.zeros_like(accf_ref)
        carry_ref[...] = jnp.zeros_like(carry_ref)
        nb = latn_ref.shape[0]
        valid = lax.broadcasted_iota(jnp.int32, (1, nb), 1) == b
        update([(latn_ref, krn_ref, fkn_ref, fvn_ref, lfn_ref)], latern_ref[...], valid)

    update(page_refs, later_ref[...], None)

    @pl.when(step == pl.num_programs(1) - 1)
    def _():
        ctx = acca_ref[...] / la_ref[...][:, 0:1]
        o_all = _dot(pad_heads(ctx), wuv_ref[...])[:n_heads]
        head = lax.broadcasted_iota(jnp.int32, o_all.shape, 0)
        seg = lax.broadcasted_iota(jnp.int32, o_all.shape, 1) // head_dim
        oa_ref[0] = jnp.sum(jnp.where(head == seg, o_all, 0.0), axis=0, keepdims=True)
        sums = jnp.sum(accf_ref[...], axis=1, keepdims=True)
        inv_l = 1.0 / lf_ref[...][:, 0:1]
        col = jnp.concatenate([sums[h * head_dim:(h + 1) * head_dim] * inv_l[h:h + 1, :]
                               for h in range(n_heads)], axis=0)
        lane = lax.broadcasted_iota(jnp.int32, of_ref.shape, 1)
        of_ref[...] = jnp.where(lane == b, col, of_ref[...])


def _sample_attention(qn, qr, qf, lat_new, kr_new_t, fk_new_t, fv_new_t, lf_new_t, wts,
                      cache_lat, cache_kr_t, cache_fk_t, cache_fv_t, cache_lf_t, page_table):
    cfg = wts["cfg"]
    nb, n_pages = page_table.shape
    page_size = cache_lat.shape[2]
    n_heads, head_dim = cfg["heads"], cfg["head_dim"]
    width = n_heads * head_dim
    hp = 16
    pps = min(SAMPLE_PAGES_PER_STEP, n_pages)
    n_steps = n_pages // pps
    qr16 = jnp.concatenate([qr, jnp.zeros((nb, hp - n_heads, qr.shape[2]), qr.dtype)], axis=1)

    per_seq = lambda shape: pl.BlockSpec((None,) + shape, lambda b, s, pt: (b,) + (0,) * len(shape))

    later = lambda n: jnp.asarray(np.tril(np.ones((n, n), np.float32), -1), BF16)
    consts = (lat_new, kr_new_t, fk_new_t, fv_new_t, lf_new_t, wts["w_uk"].T, wts["w_uv"], wts["g_kn"],
              later(page_size), later(nb))
    in_specs = [pl.BlockSpec((1, 1, width), lambda b, s, pt: (b, 0, 0)),
                per_seq(qr16.shape[1:]),
                pl.BlockSpec((1, 1, width), lambda b, s, pt: (b, 0, 0))]
    in_specs += [_const_spec(c.shape) for c in consts]
    caches = (cache_lat, cache_kr_t, cache_fk_t, cache_fv_t, cache_lf_t)
    in_specs += [pl.BlockSpec(memory_space=pl.ANY) for _ in caches]
    page_bufs = [pltpu.VMEM((SAMPLE_BUFFER_SLOTS, pps) + c.shape[2:], c.dtype) for c in caches]
    kern = functools.partial(_sample_kernel, n_heads=n_heads, head_dim=head_dim, pages_per_step=pps,
                             n_pages=n_pages)
    grid_spec = pltpu.PrefetchScalarGridSpec(
        num_scalar_prefetch=1, grid=(nb, n_steps), in_specs=in_specs,
        out_specs=(pl.BlockSpec((1, 1, width), lambda b, s, pt: (b, 0, 0)),
                   pl.BlockSpec((width, nb), lambda b, s, pt: (0, 0))),
        scratch_shapes=[pltpu.VMEM((hp + width, cfg["kv_rank"]), BF16), pltpu.VMEM((hp, width), BF16),
                        pltpu.VMEM((n_heads, LANES), F32), pltpu.VMEM((n_heads, LANES), F32),
                        pltpu.VMEM((n_heads, cfg["kv_rank"]), F32),
                        pltpu.VMEM((n_heads, LANES), F32), pltpu.VMEM((n_heads, LANES), F32),
                        pltpu.VMEM((width, max(page_size, nb)), F32),
                        pltpu.VMEM((n_heads, LANES), F32)] + page_bufs + [pltpu.SemaphoreType.DMA((SAMPLE_BUFFER_SLOTS,))])
    oa, of_t = pl.pallas_call(
        kern, grid_spec=grid_spec,
        out_shape=(jax.ShapeDtypeStruct((nb, 1, width), F32), jax.ShapeDtypeStruct((width, nb), F32)),
        compiler_params=pltpu.CompilerParams(dimension_semantics=("arbitrary", "arbitrary"),
                                             vmem_limit_bytes=VMEM_LIMIT),
        name="sample_attn",
    )(page_table.reshape(-1), qn, qr16, qf, *consts, *caches)
    return oa.reshape(nb, width), of_t.T


def _merge_kernel(x_ref, oa_ref, ob_ref, gattn_ref, wgab_ref, wpa_ref, wpb_ref, wo_ref, gffn_ref,
                  wgu_ref, wdn_ref, y_ref):
    x = x_ref[...]
    d = x.shape[1]
    h = (_rms(x) * gattn_ref[...]).astype(BF16)
    gates = _dot(h, wgab_ref[...])
    u_a = _dot(oa_ref[...].astype(BF16), wpa_ref[...])
    u_b = _dot(ob_ref[...].astype(BF16), wpb_ref[...])
    mix = jax.nn.sigmoid(gates[:, :d]) * u_a + jax.nn.sigmoid(gates[:, d:]) * u_b
    x1 = x + _dot(mix.astype(BF16), wo_ref[...])
    h2 = (_rms(x1) * gffn_ref[...]).astype(BF16)
    gu = _dot(h2, wgu_ref[...])
    d_ff = gu.shape[1] // 2
    g, u = gu[:, :d_ff], gu[:, d_ff:]
    y_ref[...] = x1 + _dot((g * jax.nn.sigmoid(g) * u).astype(BF16), wdn_ref[...])


def _merge(x, oa, ob, wts, rows):
    n, d = x.shape
    row = lambda w: pl.BlockSpec((rows, w), lambda i: (i, 0))
    weights = (wts["g_attn"], wts["w_gab"], wts["w_pa"], wts["w_pb"], wts["w_o"], wts["g_ffn"],
               wts["w_gu"], wts["w_dn"])
    return pl.pallas_call(
        _merge_kernel, grid=(n // rows,),
        in_specs=[row(d), row(oa.shape[1]), row(ob.shape[1])] + [_const_spec(w.shape, single=True) for w in weights],
        out_specs=row(d), out_shape=jax.ShapeDtypeStruct((n, d), F32),
        compiler_params=pltpu.CompilerParams(dimension_semantics=("arbitrary",), vmem_limit_bytes=VMEM_LIMIT),
        name="merge_ffn",
    )(x, oa, ob, *weights)


def _prepare(layer, cfg, norm_attn, w_in, mla_q_norm, w_uq, mla_kv_norm, w_ukv, mla_qn_nope, mla_qn_rope,
             mla_kn_nope, mla_kn_rope, fox_qn, fox_kn, fox_fb, w_pa, w_pb, w_o, norm_ffn, w_gate_up, w_down):
    d = w_in.shape[1]
    heads, nope, rope, v_dim, hd = cfg["heads"], cfg["nope"], cfg["rope"], cfg["v_dim"], cfg["head_dim"]
    q_rank, kv_rank = cfg["q_rank"], cfg["kv_rank"]
    fw = heads * hd
    sizes = (q_rank, kv_rank, rope, fw, fw, fw, heads, d, d)
    offs = np.cumsum((0,) + sizes)
    w = w_in[layer]
    part = lambda i: w[:, offs[i]:offs[i + 1]]
    zeros = lambda c: jnp.zeros((d, c), w.dtype)
    w_main = jnp.concatenate([part(0), part(1), part(3), part(4),
                              part(2), part(2), zeros(LANES - 2 * rope),
                              part(6), zeros(LANES - heads)], axis=1).astype(BF16)
    w_gab = w[:, offs[7]:offs[9]].astype(BF16)
    wq = w_uq[layer].reshape(q_rank, heads, nope + rope)
    n_pairs = heads // 2
    wq_nope = wq[:, :, :nope].reshape(q_rank, n_pairs, 2 * nope)
    wq_rope = wq[:, :, nope:].reshape(q_rank, n_pairs, 2 * rope)
    wq_pad = jnp.zeros((q_rank, n_pairs, 2 * LANES - 2 * nope - 2 * rope), w.dtype)
    w_uq_p = jnp.concatenate([wq_nope, wq_rope, wq_pad], axis=2).reshape(q_rank, n_pairs * 2 * LANES).astype(BF16)
    wkv = w_ukv[layer].reshape(kv_rank, heads, nope + v_dim)
    w_uk = wkv[:, :, :nope].reshape(kv_rank, heads * nope).astype(BF16)
    w_uv = wkv[:, :, nope:].reshape(kv_rank, heads * v_dim).astype(BF16)
    mla_scale = float(nope + rope) ** -0.5
    fox_scale = float(hd) ** -0.5
    gn, gr = mla_qn_nope[layer], mla_qn_rope[layer]
    mla_scale, fox_scale = mla_scale * LOG2E, fox_scale * LOG2E
    g_q256 = jnp.concatenate([gn, gn, gr, gr, jnp.zeros((2 * LANES - 2 * nope - 2 * rope,), F32)]) * mla_scale
    gkr = mla_kn_rope[layer]
    g_kr = jnp.concatenate([gkr, gkr, jnp.zeros((LANES - 2 * rope,), F32)])
    fb = jnp.concatenate([fox_fb[layer], jnp.zeros((LANES - heads,), F32)])
    r2 = lambda v: v.reshape(1, -1).astype(F32)
    return dict(
        cfg=cfg, g_attn=r2(norm_attn[layer]), w_main=w_main, w_gab=w_gab, g_q=r2(mla_q_norm[layer]), w_uq=w_uq_p,
        g_kv=r2(mla_kv_norm[layer]), w_uk=w_uk, w_uv=w_uv, w_uvt=w_uv.T, w_fvt=part(5).T.astype(BF16), g_q256=r2(g_q256),
        g_kn=r2(jnp.tile(mla_kn_nope[layer], heads)), g_kr=r2(g_kr),
        g_fq=r2(jnp.tile(fox_qn[layer], heads) * fox_scale), g_fk=r2(jnp.tile(fox_kn[layer], heads)), fb=r2(fb),
        w_pa=w_pa[layer].astype(BF16), w_pb=w_pb[layer].astype(BF16), w_o=w_o[layer].astype(BF16),
        g_ffn=r2(norm_ffn[layer]), w_gu=w_gate_up[layer].astype(BF16), w_dn=w_down[layer].astype(BF16))


def kernel(x_prompt, x_sample, cache_mla_latent, cache_mla_krope, cache_fox_k, cache_fox_v, cache_fox_logf, page_table, norm_attn, w_in, mla_q_norm, w_uq, mla_kv_norm, w_ukv, mla_qn_nope, mla_qn_rope, mla_kn_nope, mla_kn_rope, fox_qn, fox_kn, fox_fb, w_pa, w_pb, w_o, norm_ffn, w_gate_up, w_down):
    batch, seq, d = x_prompt.shape
    nb, dec_seq, _ = x_sample.shape
    depth, n_phys, page_size, kv_rank = cache_mla_latent.shape
    heads, hd = cache_fox_k.shape[3], cache_fox_k.shape[4]
    rope = cache_mla_krope.shape[3]
    nope = mla_qn_nope.shape[1]
    assert batch == 1 and dec_seq == 1 and depth == 1, "kernel supports one prompt sequence, one new token, one layer"
    assert w_uq.shape[2] == heads * (nope + rope) and 2 * nope == LANES and 4 * rope == LANES and hd == nope
    cfg = dict(heads=heads, head_dim=hd, nope=nope, rope=rope, v_dim=w_ukv.shape[2] // heads - nope,
               q_rank=w_uq.shape[1], kv_rank=kv_rank)
    past_len = page_table.shape[1] * page_size
    layer = 0
    wts = _prepare(layer, cfg, norm_attn, w_in, mla_q_norm, w_uq, mla_kv_norm, w_ukv, mla_qn_nope, mla_qn_rope,
                   mla_kn_nope, mla_kn_rope, fox_qn, fox_kn, fox_fb, w_pa, w_pb, w_o, norm_ffn, w_gate_up, w_down)
    fw = heads * hd
    n_pairs = heads // 2

    xp = x_prompt.reshape(seq, d)
    rows_p = min(PROJ_ROWS, seq)
    qa, ka, vat, qf, kf, vft, lat_p, krt_p, fkt_p, fvt_p, lft_p = _project(xp, jnp.arange(seq), wts, rows_p)
    oa_p = _attention(qa, ka, vat, head_dim=nope, aug_width=rope, aug_per_pair=True, name="attn_mla")
    ob_p = _attention(qf, kf, vft, head_dim=hd, aug_width=FORGET_GROUP, aug_per_pair=False, name="attn_fox")
    y_p = _merge(xp, oa_p, ob_p, wts, rows_p).reshape(batch, seq, d)

    xs = x_sample.reshape(nb, d)
    pos_s = jnp.full((nb,), past_len, jnp.int32)
    qa_s, _, _, qf_s, _, _, lat_s, krt_s, fkt_s, fvt_s, lft_s = _project(xs, pos_s, wts, nb)
    qa3 = qa_s.reshape(nb, n_pairs, 2 * LANES)
    qn = qa3[:, :, :LANES].reshape(nb, 1, fw)
    qr = qa3[:, :, LANES:LANES + 2 * rope].reshape(nb, heads, rope)
    qfn = qf_s.reshape(nb, n_pairs, 2 * LANES)[:, :, :LANES].reshape(nb, 1, fw)
    cache_kr_t = jnp.transpose(cache_mla_krope, (0, 1, 3, 2))
    cache_fk_t = jnp.transpose(cache_fox_k, (0, 1, 3, 4, 2)).reshape(depth, n_phys, fw, page_size)
    cache_fv_t = jnp.transpose(cache_fox_v, (0, 1, 3, 4, 2)).reshape(depth, n_phys, fw, page_size)
    cache_lf_t = jnp.transpose(cache_fox_logf, (0, 1, 3, 2))
    oa_s, ob_s = _sample_attention(qn, qr, qfn, lat_s, krt_s, fkt_s, fvt_s, lft_s, wts,
                                   cache_mla_latent, cache_kr_t, cache_fk_t, cache_fv_t, cache_lf_t, page_table)
    y_s = _merge(xs, oa_s, ob_s, wts, nb).reshape(nb, dec_seq, d)

    def cache_outputs(lat, krt, fkt, fvt, lft, lead):
        n = lat.shape[0]
        shape = lambda *tail: (depth,) + lead + tail
        return (lat.reshape(shape(kv_rank)),
                krt.T.reshape(shape(rope)),
                jnp.transpose(fkt.reshape(heads, hd, n), (2, 0, 1)).reshape(shape(heads, hd)),
                jnp.transpose(fvt.reshape(heads, hd, n), (2, 0, 1)).reshape(shape(heads, hd)),
                lft.T.reshape(shape(heads)))

    return (y_p, y_s) + cache_outputs(lat_p, krt_p, fkt_p, fvt_p, lft_p, (batch, seq)) \
        + cache_outputs(lat_s, krt_s, fkt_s, fvt_s, lft_s, (nb, dec_seq))
```

```python
import functools

import numpy as np
import jax
import jax.numpy as jnp
from jax import lax
from jax.experimental import pallas as pl
from jax.experimental.pallas import tpu as pltpu

F32 = jnp.float32
BF16 = jnp.bfloat16

EPS = 1e-6
NEG_INF = -1e30
ROPE_THETA = 10000.0
LOG2E = 1.4426950408889634
FORGET_GROUP = 8
LANES = 128
VMEM_LIMIT = 56 * 1024 * 1024

PROJ_ROWS = 256
ATTN_TQ = 512
ATTN_TK = 1024
SAMPLE_PAGES_PER_STEP = 8
SAMPLE_BUFFER_SLOTS = 3


def _dot(a, b):
    return jnp.dot(a, b, preferred_element_type=F32)


def _dot_nt(a, b):
    return lax.dot_general(a, b, (((1,), (1,)), ((), ())), preferred_element_type=F32)


def _rms(x):
    return x * lax.rsqrt(jnp.mean(x * x, axis=-1, keepdims=True) + EPS)


def _seg_norm(x, seg_mean):
    ms = _dot((x * x).astype(BF16), seg_mean)
    return x * lax.rsqrt(ms + EPS)


def _rope(t, cos_t, sin_a, sin_b):
    half = 16
    return t * cos_t + pltpu.roll(t, LANES - half, 1) * sin_a + pltpu.roll(t, half, 1) * sin_b


def _split3(x):
    hi = x.astype(BF16)
    r = x - hi.astype(F32)
    mid = r.astype(BF16)
    lo = (r - mid.astype(F32)).astype(BF16)
    return hi, mid, lo


def _proj_kernel(x_ref, cos_ref, sina_ref, sinb_ref, gattn_ref, wmain_ref, gq_ref, wuq_ref, gkv_ref,
                 wuk_ref, wuvt_ref, wfvt_ref, gq256_ref, gkn_ref, gkr_ref, gfq_ref, gfk_ref, fb_ref,
                 bq_ref, b64_ref, br_ref, ltri_ref, pcum_ref,
                 qa_ref, ka_ref, vat_ref, qf_ref, kf_ref, vft_ref,
                 lat_ref, krt_ref, fkt_ref, fvt_ref, lft_ref,
                 carry_ref, *, q_rank, kv_rank, fox_width, n_pairs):
    @pl.when(pl.program_id(0) == 0)
    def _():
        carry_ref[...] = jnp.zeros_like(carry_ref)

    cos_t, sin_a, sin_b = cos_ref[...], sina_ref[...], sinb_ref[...]
    h = (_rms(x_ref[...]) * gattn_ref[...]).astype(BF16)

    o_kv = q_rank
    o_fq = o_kv + kv_rank
    o_fk = o_fq + fox_width
    o_kr = o_fk + fox_width
    o_ff = o_kr + LANES

    c_q = _dot(h, wmain_ref[:, 0:o_kv])
    cqn = (_rms(c_q) * gq_ref[...]).astype(BF16)
    q_all = _dot(cqn, wuq_ref[...])
    for j in range(n_pairs):
        blk = q_all[:, 2 * LANES * j:2 * LANES * (j + 1)]
        blk = _seg_norm(blk, bq_ref[...]) * gq256_ref[...]
        rope = _rope(blk[:, LANES:], cos_t, sin_a, sin_b)
        qa_ref[:, 2 * LANES * j:2 * LANES * j + LANES] = blk[:, :LANES].astype(BF16)
        qa_ref[:, 2 * LANES * j + LANES:2 * LANES * (j + 1)] = rope.astype(BF16)

    c_kv = _dot(h, wmain_ref[:, o_kv:o_fq])
    latent = _rms(c_kv) * gkv_ref[...]
    lat_ref[...] = latent
    lat_b = latent.astype(BF16)
    kn = _seg_norm(_dot(lat_b, wuk_ref[...]), b64_ref[...]) * gkn_ref[...]
    vat_ref[...] = _dot_nt(wuvt_ref[...], lat_b).astype(BF16)
    kr = _dot(h, wmain_ref[:, o_kr:o_ff])
    kr = _rope(_seg_norm(kr, br_ref[...]) * gkr_ref[...], cos_t, sin_a, sin_b)
    krt_ref[...] = kr.T[0:krt_ref.shape[0], :]
    kr_b = kr.astype(BF16)
    kn_b = kn.astype(BF16)
    for j in range(n_pairs):
        ka_ref[:, 2 * LANES * j:2 * LANES * j + LANES] = kn_b[:, LANES * j:LANES * (j + 1)]
        ka_ref[:, 2 * LANES * j + LANES:2 * LANES * (j + 1)] = kr_b

    fq = _seg_norm(_dot(h, wmain_ref[:, o_fq:o_fk]), b64_ref[...]) * gfq_ref[...]
    fk = _seg_norm(_dot(h, wmain_ref[:, o_fk:o_kr]), b64_ref[...]) * gfk_ref[...]
    fkt_ref[...] = fk.T
    fv_t = _dot_nt(wfvt_ref[...], h)
    fvt_ref[...] = fv_t
    vft_ref[...] = fv_t.astype(BF16)

    ff = _dot(h, wmain_ref[:, o_ff:o_ff + LANES]) + fb_ref[...]
    logf = jnp.minimum(ff, 0.0) - jnp.log1p(jnp.exp(-jnp.abs(ff)))
    lane = lax.broadcasted_iota(jnp.int32, logf.shape, 1)
    logf = jnp.where(lane < lft_ref.shape[0], logf, 0.0)
    lft_ref[...] = logf.T[0:lft_ref.shape[0], :]
    l_hi, l_mid, l_lo = _split3(logf)
    ltri = ltri_ref[...]
    cum = carry_ref[...] + (_dot(ltri, l_hi) + _dot(ltri, l_mid) + _dot(ltri, l_lo))
    carry_ref[...] = cum[cum.shape[0] - 1:, :]
    c_hi, c_mid, c_lo = _split3(cum * LOG2E)
    packed = (c_hi.astype(F32) + pltpu.roll(c_mid.astype(F32), FORGET_GROUP, 1)
              + pltpu.roll(c_lo.astype(F32), 2 * FORGET_GROUP, 1)
              + jnp.where(lane == 3 * FORGET_GROUP, 1.0, 0.0))
    aug = _dot(packed.astype(BF16), pcum_ref[...]).astype(BF16)
    fq_b = fq.astype(BF16)
    fk_b = fk.astype(BF16)
    for j in range(n_pairs):
        qf_ref[:, 2 * LANES * j:2 * LANES * j + LANES] = fq_b[:, LANES * j:LANES * (j + 1)]
        qf_ref[:, 2 * LANES * j + LANES:2 * LANES * (j + 1)] = aug[:, :LANES]
        kf_ref[:, 2 * LANES * j:2 * LANES * j + LANES] = fk_b[:, LANES * j:LANES * (j + 1)]
        kf_ref[:, 2 * LANES * j + LANES:2 * LANES * (j + 1)] = aug[:, LANES:]


def _block_diag(sizes_scales, n):
    m = np.zeros((n, n), np.float32)
    o = 0
    for size, scale in sizes_scales:
        m[o:o + size, o:o + size] = scale
        o += size
    return m


def _proj_consts(rows, n_heads, nope, rope_dim, head_dim):
    bq = _block_diag([(nope, 1.0 / nope)] * 2 + [(rope_dim, 1.0 / rope_dim)] * 2, 2 * LANES)
    b64 = _block_diag([(head_dim, 1.0 / head_dim)] * n_heads, n_heads * head_dim)
    br = _block_diag([(rope_dim, 1.0 / rope_dim)] * 2, LANES)
    ltri = np.tril(np.ones((rows, rows), np.float32))
    g = FORGET_GROUP
    pcum = np.zeros((LANES, 2 * LANES), np.float32)
    for h in range(n_heads):
        for part in range(3):
            pcum[part * g + h, g * h + part] = 1.0
            pcum[3 * g, g * h + 3 + part] = 1.0
            pcum[3 * g, LANES + g * h + part] = 1.0
            pcum[part * g + h, LANES + g * h + 3 + part] = -1.0
    return tuple(jnp.asarray(a, BF16) for a in (bq, b64, br, ltri, pcum))


def _rope_tables(pos):
    half = 16
    inv = ROPE_THETA ** (-jnp.arange(half, dtype=F32) / half)
    ang = pos.astype(F32)[:, None] * inv[None, :]
    c = jnp.tile(jnp.cos(ang), (1, LANES // half))
    s = jnp.tile(jnp.sin(ang), (1, LANES // half))
    piece = np.arange(LANES) // half
    live = piece < 4
    sel = lambda m: jnp.asarray(m.astype(np.float32))[None, :]
    return (c * sel(live), s * sel(-1.0 * (live & (piece % 2 == 0))), s * sel(live & (piece % 2 == 1)))


def _const_spec(shape, single=False):
    nd = len(shape)
    if single:
        return pl.BlockSpec(shape, lambda *_: (0,) * nd, pipeline_mode=pl.Buffered(1))
    return pl.BlockSpec(shape, lambda *_: (0,) * nd)


def _project(x, pos, wts, rows):
    n, d = x.shape
    cfg = wts["cfg"]
    n_pairs = cfg["heads"] // 2
    fox_width = cfg["heads"] * cfg["head_dim"]
    tables = _rope_tables(pos)
    consts = _proj_consts(rows, cfg["heads"], cfg["nope"], cfg["rope"], cfg["head_dim"])
    row = lambda w: pl.BlockSpec((rows, w), lambda i: (i, 0))
    col = lambda r: pl.BlockSpec((r, rows), lambda i: (0, i))
    weights = (wts["g_attn"], wts["w_main"], wts["g_q"], wts["w_uq"], wts["g_kv"], wts["w_uk"], wts["w_uvt"],
               wts["w_fvt"], wts["g_q256"], wts["g_kn"], wts["g_kr"], wts["g_fq"], wts["g_fk"], wts["fb"]) + consts
    in_specs = [row(d), row(LANES), row(LANES), row(LANES)] + [_const_spec(w.shape) for w in weights]
    pair_w = 2 * LANES * n_pairs
    out_shape = (
        jax.ShapeDtypeStruct((n, pair_w), BF16), jax.ShapeDtypeStruct((n, pair_w), BF16),
        jax.ShapeDtypeStruct((fox_width, n), BF16),
        jax.ShapeDtypeStruct((n, pair_w), BF16), jax.ShapeDtypeStruct((n, pair_w), BF16),
        jax.ShapeDtypeStruct((fox_width, n), BF16),
        jax.ShapeDtypeStruct((n, cfg["kv_rank"]), F32),
        jax.ShapeDtypeStruct((cfg["rope"], n), F32),
        jax.ShapeDtypeStruct((fox_width, n), F32), jax.ShapeDtypeStruct((fox_width, n), F32),
        jax.ShapeDtypeStruct((cfg["heads"], n), F32),
    )
    out_specs = (row(pair_w), row(pair_w), col(fox_width), row(pair_w), row(pair_w), col(fox_width),
                 row(cfg["kv_rank"]), col(cfg["rope"]), col(fox_width), col(fox_width), col(cfg["heads"]))
    kern = functools.partial(_proj_kernel, q_rank=cfg["q_rank"], kv_rank=cfg["kv_rank"],
                             fox_width=fox_width, n_pairs=n_pairs)
    return pl.pallas_call(
        kern, grid=(n // rows,), in_specs=in_specs, out_specs=out_specs, out_shape=out_shape,
        scratch_shapes=[pltpu.VMEM((1, LANES), F32)],
        compiler_params=pltpu.CompilerParams(dimension_semantics=("arbitrary",), vmem_limit_bytes=VMEM_LIMIT),
        name="proj",
    )(x, *tables, *weights)


def _attn_kernel(q_ref, k_ref, vt_ref, o_ref, st_ref, *, tq, tk, head_dim, aug_width, aug_per_pair):
    pair = pl.program_id(0)
    qi = pl.program_id(1)
    q = q_ref[...]
    lane = lax.broadcasted_iota(jnp.int32, (1, q.shape[1]), 1)
    qm = []
    for half in range(2):
        if aug_per_pair:
            a0 = LANES + aug_width * half
        else:
            a0 = LANES + aug_width * (2 * pair + half)
        keep = ((lane >= head_dim * half) & (lane < head_dim * (half + 1))) | ((lane >= a0) & (lane < a0 + aug_width))
        qm.append(jnp.where(keep, q, jnp.zeros_like(q)))

    def scores(start, size):
        k = k_ref[pl.ds(start, size), :]
        return [_dot_nt(k, qm[half]) for half in range(2)]

    def softmax_pv(st, start, size, carry, masked):
        vt = vt_ref[:, pl.ds(start, size)]
        if masked:
            key = start + lax.broadcasted_iota(jnp.int32, (size, tq), 0)
            qry = qi * tq + lax.broadcasted_iota(jnp.int32, (size, tq), 1)
            causal = key <= qry
        out = []
        for half in range(2):
            m, l, acc = carry[half]
            s = st[half]
            if masked:
                s = jnp.where(causal, s, NEG_INF)
            m_new = jnp.maximum(m, jnp.max(s, axis=0, keepdims=True))
            alpha = jnp.exp2(m - m_new)
            p = jnp.exp2(s - m_new)
            l = alpha * l + jnp.sum(p, axis=0, keepdims=True)
            acc = alpha * acc + _dot(vt[half * head_dim:(half + 1) * head_dim, :], p.astype(BF16))
            out.append((m_new, l, acc))
        return tuple(out)

    def step(start, size, carry, masked):
        return softmax_pv(scores(start, size), start, size, carry, masked)

    def scores_to(slot, block):
        st = scores(pl.multiple_of(block * tk, tk), tk)
        for half in range(2):
            st_ref[slot, half] = st[half]

    def softmax_from(slot, block, carry):
        st = [st_ref[slot, half] for half in range(2)]
        return softmax_pv(st, pl.multiple_of(block * tk, tk), tk, carry, False)

    carry = tuple((jnp.full((1, tq), NEG_INF, F32), jnp.zeros((1, tq), F32), jnp.zeros((head_dim, tq), F32))
                  for _ in range(2))
    n_big = (qi * tq) // tk
    n_small = qi - n_big * (tk // tq)

    scores_to(0, 0)

    def pair_body(i, c):
        scores_to(1, 2 * i + 1)
        c = softmax_from(0, 2 * i, c)
        scores_to(0, 2 * i + 2)
        return softmax_from(1, 2 * i + 1, c)

    carry = lax.fori_loop(0, n_big // 2, pair_body, carry)
    carry = lax.cond(n_big % 2 == 1, lambda c: softmax_from(0, n_big - 1, c), lambda c: c, carry)
    if tk > tq:
        carry = lax.fori_loop(
            0, n_small, lambda i, c: step(pl.multiple_of(n_big * tk + i * tq, tq), tq, c, False), carry)
    carry = step(pl.multiple_of(qi * tq, tq), tq, carry, True)
    (_, l_e, acc_e), (_, l_o, acc_o) = carry
    o_t = jnp.concatenate([acc_e / l_e, acc_o / l_o], axis=0)
    o_ref[...] = o_t.T.astype(o_ref.dtype)


def _attention(q_cat, k_cat, v_t, *, head_dim, aug_width, aug_per_pair, name):
    s = q_cat.shape[0]
    n_pairs = q_cat.shape[1] // (2 * LANES)
    tq, tk = min(ATTN_TQ, s), min(ATTN_TK, s)
    assert s % tk == 0 and tk % tq == 0
    kern = functools.partial(_attn_kernel, tq=tq, tk=tk, head_dim=head_dim, aug_width=aug_width,
                             aug_per_pair=aug_per_pair)
    return pl.pallas_call(
        kern, grid=(n_pairs, s // tq),
        in_specs=[pl.BlockSpec((tq, 2 * LANES), lambda j, i: (i, j)),
                  pl.BlockSpec((s, 2 * LANES), lambda j, i: (0, j)),
                  pl.BlockSpec((2 * head_dim, s), lambda j, i: (j, 0))],
        out_specs=pl.BlockSpec((tq, 2 * head_dim), lambda j, i: (i, j)),
        out_shape=jax.ShapeDtypeStruct((s, n_pairs * 2 * head_dim), BF16),
        scratch_shapes=[pltpu.VMEM((2, 2, tk, tq), F32)],
        compiler_params=pltpu.CompilerParams(dimension_semantics=("arbitrary", "arbitrary"),
                                             vmem_limit_bytes=VMEM_LIMIT),
        name=name,
    )(q_cat, k_cat, v_t)


def _sample_kernel(pt_ref, qn_ref, qr_ref, qf_ref, latn_ref, krn_ref, fkn_ref, fvn_ref, lfn_ref,
                   wukt_ref, wuv_ref, gkn_ref, later_ref, latern_ref,
                   clat_hbm, ckr_hbm, cfk_hbm, cfv_hbm, clf_hbm, oa_ref, of_ref,
                   lhs_ref, qfb_ref, ma_ref, la_ref, acca_ref, mf_ref, lf_ref, accf_ref, carry_ref,
                   lat_buf, kr_buf, fk_buf, fv_buf, lf_buf, sem,
                   *, n_heads, head_dim, pages_per_step, n_pages):
    b = pl.program_id(0)
    step = pl.program_id(1)
    n_steps = pl.num_programs(1)
    width = n_heads * head_dim
    hp = lhs_ref.shape[0] - width

    caches = ((clat_hbm, lat_buf), (ckr_hbm, kr_buf), (cfk_hbm, fk_buf), (cfv_hbm, fv_buf), (clf_hbm, lf_buf))
    n_slots = lat_buf.shape[0]
    ahead = n_slots - 1

    def page_copies(g):
        seq, stp, slot = g // n_steps, g % n_steps, g % n_slots
        out = []
        for i in range(pages_per_step):
            page = pt_ref[seq * n_pages + (n_pages - 1 - (stp * pages_per_step + i))]
            out += [pltpu.make_async_copy(hbm.at[0, page], buf.at[slot, i], sem.at[slot]) for hbm, buf in caches]
        return out

    lin = b * n_steps + step
    slot = lin % n_slots
    total = pl.num_programs(0) * n_steps

    @pl.when(lin == 0)
    def _():
        for g in range(ahead):
            @pl.when(g < total)
            def _():
                for c in page_copies(g):
                    c.start()

    @pl.when(lin + ahead < total)
    def _():
        for c in page_copies(lin + ahead):
            c.start()

    for c in page_copies(lin):
        c.wait()
    page_refs = [tuple(buf.at[slot, i] for _, buf in caches) for i in range(pages_per_step)]

    def pad_heads(x):
        return jnp.concatenate([x, jnp.zeros((hp - n_heads, x.shape[1]), F32)], axis=0).astype(BF16)

    def update(pages, later, valid):
        n = len(pages)
        w = pages[0][0].shape[0]
        lat_b = jnp.concatenate([pg[0][...].astype(BF16) for pg in pages], axis=0)
        kr_b = jnp.concatenate([pg[1][...].astype(BF16) for pg in pages], axis=1)
        fk_b = jnp.concatenate([pg[2][...].astype(BF16) for pg in pages], axis=1)
        both = _dot_nt(lhs_ref[...], lat_b)
        numer = both[:n_heads]
        sq = both[hp:] * both[hp:]
        msq = jnp.concatenate([jnp.sum(sq[h * head_dim:(h + 1) * head_dim], axis=0, keepdims=True)
                               for h in range(n_heads)], axis=0) * (1.0 / head_dim)
        rope = _dot(qr_ref[...], kr_b)[:n_heads]
        s_a = numer * lax.rsqrt(msq + EPS) + rope
        s_f = _dot(qfb_ref[...], fk_b)[:n_heads]

        parts, tots = [], []
        for pg in pages:
            lf_t = pg[4][...]
            if valid is not None:
                lf_t = jnp.where(valid, lf_t, 0.0)
            hi = lf_t.astype(BF16).astype(F32)
            parts += [hi, lf_t - hi]
            tots.append(jnp.sum(lf_t, axis=1, keepdims=True))
        suf = _dot(jnp.concatenate(parts, axis=0).astype(BF16), later)
        carry = carry_ref[...][:, 0:1]
        bias = []
        for i in range(n):
            r = 2 * n_heads * i
            bias.append(carry + suf[r:r + n_heads] + suf[r + n_heads:r + 2 * n_heads])
            carry = carry + tots[i]
        carry_ref[...] = jnp.broadcast_to(carry, carry_ref.shape)
        s_f = s_f + jnp.concatenate(bias, axis=1) * LOG2E
        if valid is not None:
            s_a = jnp.where(valid, s_a, NEG_INF)
            s_f = jnp.where(valid, s_f, NEG_INF)

        m_old = ma_ref[...][:, 0:1]
        m_new = jnp.maximum(m_old, jnp.max(s_a, axis=1, keepdims=True))
        alpha = jnp.exp2(m_old - m_new)
        p = jnp.exp2(s_a - m_new)
        la_ref[...] = alpha * la_ref[...] + jnp.sum(p, axis=1, keepdims=True)
        ma_ref[...] = jnp.broadcast_to(m_new, ma_ref.shape)
        acca_ref[...] = alpha * acca_ref[...] + _dot(pad_heads(p), lat_b)[:n_heads]

        m_old = mf_ref[...][:, 0:1]
        m_new = jnp.maximum(m_old, jnp.max(s_f, axis=1, keepdims=True))
        alpha = jnp.exp2(m_old - m_new)
        p = jnp.exp2(s_f - m_new)
        lf_ref[...] = alpha * lf_ref[...] + jnp.sum(p, axis=1, keepdims=True)
        mf_ref[...] = jnp.broadcast_to(m_new, mf_ref.shape)
        for h in range(n_heads):
            rows = slice(h * head_dim, (h + 1) * head_dim)
            acc = alpha[h:h + 1, :] * accf_ref[rows, 0:w]
            for i, pg in enumerate(pages):
                acc = acc + p[h:h + 1, i * w:(i + 1) * w] * pg[3][rows, :]
            accf_ref[rows, 0:w] = acc

    @pl.when((step == 0) & (b == 0))
    def _():
        of_ref[...] = jnp.zeros_like(of_ref)

    @pl.when(step == 0)
    def _():
        head = lax.broadcasted_iota(jnp.int32, (hp, width), 0)
        seg = lax.broadcasted_iota(jnp.int32, (hp, width), 1) // head_dim
        own = head == seg
        qg = qn_ref[0].astype(F32) * gkn_ref[...]
        q_bd = jnp.where(own, jnp.broadcast_to(qg, (hp, width)), 0.0).astype(BF16)
        lhs_ref[0:hp, :] = _dot(q_bd, wukt_ref[...]).astype(BF16)
        lhs_ref[hp:, :] = wukt_ref[...]
        qfb_ref[...] = jnp.where(own, jnp.broadcast_to(qf_ref[0].astype(F32), (hp, width)), 0.0).astype(BF16)
        ma_ref[...] = jnp.full_like(ma_ref, NEG_INF)
        mf_ref[...] = jnp.full_like(mf_ref, NEG_INF)
        la_ref[...] = jnp.zeros_like(la_ref)
        lf_ref[...] = jnp.zeros_like(lf_ref)
        acca_ref[...] = jnp.zeros_like(acca_ref)
        accf_ref[...] = jnp.zeros_like(accf_ref)
        carry_ref[...] = jnp.zeros_like(carry_ref)
        nb = latn_ref.shape[0]
        valid = lax.broadcasted_iota(jnp.int32, (1, nb), 1) == b
        update([(latn_ref, krn_ref, fkn_ref, fvn_ref, lfn_ref)], latern_ref[...], valid)

    update(page_refs, later_ref[...], None)

    @pl.when(step == pl.num_programs(1) - 1)
    def _():
        ctx = acca_ref[...] / la_ref[...][:, 0:1]
        o_all = _dot(pad_heads(ctx), wuv_ref[...])[:n_heads]
        head = lax.broadcasted_iota(jnp.int32, o_all.shape, 0)
        seg = lax.broadcasted_iota(jnp.int32, o_all.shape, 1) // head_dim
        oa_ref[0] = jnp.sum(jnp.where(head == seg, o_all, 0.0), axis=0, keepdims=True)
        sums = jnp.sum(accf_ref[...], axis=1, keepdims=True)
        inv_l = 1.0 / lf_ref[...][:, 0:1]
        col = jnp.concatenate([sums[h * head_dim:(h + 1) * head_dim] * inv_l[h:h + 1, :]
                               for h in range(n_heads)], axis=0)
        lane = lax.broadcasted_iota(jnp.int32, of_ref.shape, 1)
        of_ref[...] = jnp.where(lane == b, col, of_ref[...])


def _sample_attention(qn, qr, qf, lat_new, kr_new_t, fk_new_t, fv_new_t, lf_new_t, wts,
                      cache_lat, cache_kr_t, cache_fk_t, cache_fv_t, cache_lf_t, page_table):
    cfg = wts["cfg"]
    nb, n_pages = page_table.shape
    page_size = cache_lat.shape[2]
    n_heads, head_dim = cfg["heads"], cfg["head_dim"]
    width = n_heads * head_dim
    hp = 16
    pps = min(SAMPLE_PAGES_PER_STEP, n_pages)
    n_steps = n_pages // pps
    qr16 = jnp.concatenate([qr, jnp.zeros((nb, hp - n_heads, qr.shape[2]), qr.dtype)], axis=1)

    per_seq = lambda shape: pl.BlockSpec((None,) + shape, lambda b, s, pt: (b,) + (0,) * len(shape))

    later = lambda n: jnp.asarray(np.tril(np.ones((n, n), np.float32), -1), BF16)
    consts = (lat_new, kr_new_t, fk_new_t, fv_new_t, lf_new_t, wts["w_uk"].T, wts["w_uv"], wts["g_kn"],
              later(page_size), later(nb))
    in_specs = [pl.BlockSpec((1, 1, width), lambda b, s, pt: (b, 0, 0)),
                per_seq(qr16.shape[1:]),
                pl.BlockSpec((1, 1, width), lambda b, s, pt: (b, 0, 0))]
    in_specs += [_const_spec(c.shape) for c in consts]
    caches = (cache_lat, cache_kr_t, cache_fk_t, cache_fv_t, cache_lf_t)
    in_specs += [pl.BlockSpec(memory_space=pl.ANY) for _ in caches]
    page_bufs = [pltpu.VMEM((SAMPLE_BUFFER_SLOTS, pps) + c.shape[2:], c.dtype) for c in caches]
    kern = functools.partial(_sample_kernel, n_heads=n_heads, head_dim=head_dim, pages_per_step=pps,
                             n_pages=n_pages)
    grid_spec = pltpu.PrefetchScalarGridSpec(
        num_scalar_prefetch=1, grid=(nb, n_steps), in_specs=in_specs,
        out_specs=(pl.BlockSpec((1, 1, width), lambda b, s, pt: (b, 0, 0)),
                   pl.BlockSpec((width, nb), lambda b, s, pt: (0, 0))),
        scratch_shapes=[pltpu.VMEM((hp + width, cfg["kv_rank"]), BF16), pltpu.VMEM((hp, width), BF16),
                        pltpu.VMEM((n_heads, LANES), F32), pltpu.VMEM((n_heads, LANES), F32),
                        pltpu.VMEM((n_heads, cfg["kv_rank"]), F32),
                        pltpu.VMEM((n_heads, LANES), F32), pltpu.VMEM((n_heads, LANES), F32),
                        pltpu.VMEM((width, max(page_size, nb)), F32),
                        pltpu.VMEM((n_heads, LANES), F32)] + page_bufs + [pltpu.SemaphoreType.DMA((SAMPLE_BUFFER_SLOTS,))])
    oa, of_t = pl.pallas_call(
        kern, grid_spec=grid_spec,
        out_shape=(jax.ShapeDtypeStruct((nb, 1, width), F32), jax.ShapeDtypeStruct((width, nb), F32)),
        compiler_params=pltpu.CompilerParams(dimension_semantics=("arbitrary", "arbitrary"),
                                             vmem_limit_bytes=VMEM_LIMIT),
        name="sample_attn",
    )(page_table.reshape(-1), qn, qr16, qf, *consts, *caches)
    return oa.reshape(nb, width), of_t.T


def _merge_kernel(x_ref, oa_ref, ob_ref, gattn_ref, wgab_ref, wpa_ref, wpb_ref, wo_ref, gffn_ref,
                  wgu_ref, wdn_ref, y_ref):
    x = x_ref[...]
    d = x.shape[1]
    h = (_rms(x) * gattn_ref[...]).astype(BF16)
    gates = _dot(h, wgab_ref[...])
    u_a = _dot(oa_ref[...].astype(BF16), wpa_ref[...])
    u_b = _dot(ob_ref[...].astype(BF16), wpb_ref[...])
    mix = jax.nn.sigmoid(gates[:, :d]) * u_a + jax.nn.sigmoid(gates[:, d:]) * u_b
    x1 = x + _dot(mix.astype(BF16), wo_ref[...])
    h2 = (_rms(x1) * gffn_ref[...]).astype(BF16)
    gu = _dot(h2, wgu_ref[...])
    d_ff = gu.shape[1] // 2
    g, u = gu[:, :d_ff], gu[:, d_ff:]
    y_ref[...] = x1 + _dot((g * jax.nn.sigmoid(g) * u).astype(BF16), wdn_ref[...])


def _merge(x, oa, ob, wts, rows):
    n, d = x.shape
    row = lambda w: pl.BlockSpec((rows, w), lambda i: (i, 0))
    weights = (wts["g_attn"], wts["w_gab"], wts["w_pa"], wts["w_pb"], wts["w_o"], wts["g_ffn"],
               wts["w_gu"], wts["w_dn"])
    return pl.pallas_call(
        _merge_kernel, grid=(n // rows,),
        in_specs=[row(d), row(oa.shape[1]), row(ob.shape[1])] + [_const_spec(w.shape, single=True) for w in weights],
        out_specs=row(d), out_shape=jax.ShapeDtypeStruct((n, d), F32),
        compiler_params=pltpu.CompilerParams(dimension_semantics=("arbitrary",), vmem_limit_bytes=VMEM_LIMIT),
        name="merge_ffn",
    )(x, oa, ob, *weights)


def _prepare(layer, cfg, norm_attn, w_in, mla_q_norm, w_uq, mla_kv_norm, w_ukv, mla_qn_nope, mla_qn_rope,
             mla_kn_nope, mla_kn_rope, fox_qn, fox_kn, fox_fb, w_pa, w_pb, w_o, norm_ffn, w_gate_up, w_down):
    d = w_in.shape[1]
    heads, nope, rope, v_dim, hd = cfg["heads"], cfg["nope"], cfg["rope"], cfg["v_dim"], cfg["head_dim"]
    q_rank, kv_rank = cfg["q_rank"], cfg["kv_rank"]
    fw = heads * hd
    sizes = (q_rank, kv_rank, rope, fw, fw, fw, heads, d, d)
    offs = np.cumsum((0,) + sizes)
    w = w_in[layer]
    part = lambda i: w[:, offs[i]:offs[i + 1]]
    zeros = lambda c: jnp.zeros((d, c), w.dtype)
    w_main = jnp.concatenate([part(0), part(1), part(3), part(4),
                              part(2), part(2), zeros(LANES - 2 * rope),
                              part(6), zeros(LANES - heads)], axis=1).astype(BF16)
    w_gab = w[:, offs[7]:offs[9]].astype(BF16)
    wq = w_uq[layer].reshape(q_rank, heads, nope + rope)
    n_pairs = heads // 2
    wq_nope = wq[:, :, :nope].reshape(q_rank, n_pairs, 2 * nope)
    wq_rope = wq[:, :, nope:].reshape(q_rank, n_pairs, 2 * rope)
    wq_pad = jnp.zeros((q_rank, n_pairs, 2 * LANES - 2 * nope - 2 * rope), w.dtype)
    w_uq_p = jnp.concatenate([wq_nope, wq_rope, wq_pad], axis=2).reshape(q_rank, n_pairs * 2 * LANES).astype(BF16)
    wkv = w_ukv[layer].reshape(kv_rank, heads, nope + v_dim)
    w_uk = wkv[:, :, :nope].reshape(kv_rank, heads * nope).astype(BF16)
    w_uv = wkv[:, :, nope:].reshape(kv_rank, heads * v_dim).astype(BF16)
    mla_scale = float(nope + rope) ** -0.5
    fox_scale = float(hd) ** -0.5
    gn, gr = mla_qn_nope[layer], mla_qn_rope[layer]
    mla_scale, fox_scale = mla_scale * LOG2E, fox_scale * LOG2E
    g_q256 = jnp.concatenate([gn, gn, gr, gr, jnp.zeros((2 * LANES - 2 * nope - 2 * rope,), F32)]) * mla_scale
    gkr = mla_kn_rope[layer]
    g_kr = jnp.concatenate([gkr, gkr, jnp.zeros((LANES - 2 * rope,), F32)])
    fb = jnp.concatenate([fox_fb[layer], jnp.zeros((LANES - heads,), F32)])
    r2 = lambda v: v.reshape(1, -1).astype(F32)
    return dict(
        cfg=cfg, g_attn=r2(norm_attn[layer]), w_main=w_main, w_gab=w_gab, g_q=r2(mla_q_norm[layer]), w_uq=w_uq_p,
        g_kv=r2(mla_kv_norm[layer]), w_uk=w_uk, w_uv=w_uv, w_uvt=w_uv.T, w_fvt=part(5).T.astype(BF16), g_q256=r2(g_q256),
        g_kn=r2(jnp.tile(mla_kn_nope[layer], heads)), g_kr=r2(g_kr),
        g_fq=r2(jnp.tile(fox_qn[layer], heads) * fox_scale), g_fk=r2(jnp.tile(fox_kn[layer], heads)), fb=r2(fb),
        w_pa=w_pa[layer].astype(BF16), w_pb=w_pb[layer].astype(BF16), w_o=w_o[layer].astype(BF16),
        g_ffn=r2(norm_ffn[layer]), w_gu=w_gate_up[layer].astype(BF16), w_dn=w_down[layer].astype(BF16))


def kernel(x_prompt, x_sample, cache_mla_latent, cache_mla_krope, cache_fox_k, cache_fox_v, cache_fox_logf, page_table, norm_attn, w_in, mla_q_norm, w_uq, mla_kv_norm, w_ukv, mla_qn_nope, mla_qn_rope, mla_kn_nope, mla_kn_rope, fox_qn, fox_kn, fox_fb, w_pa, w_pb, w_o, norm_ffn, w_gate_up, w_down):
    batch, seq, d = x_prompt.shape
    nb, dec_seq, _ = x_sample.shape
    depth, n_phys, page_size, kv_rank = cache_mla_latent.shape
    heads, hd = cache_fox_k.shape[3], cache_fox_k.shape[4]
    rope = cache_mla_krope.shape[3]
    nope = mla_qn_nope.shape[1]
    assert batch == 1 and dec_seq == 1 and depth == 1, "kernel supports one prompt sequence, one new token, one layer"
    assert w_uq.shape[2] == heads * (nope + rope) and 2 * nope == LANES and 4 * rope == LANES and hd == nope
    cfg = dict(heads=heads, head_dim=hd, nope=nope, rope=rope, v_dim=w_ukv.shape[2] // heads - nope,
               q_rank=w_uq.shape[1], kv_rank=kv_rank)
    past_len = page_table.shape[1] * page_size
    layer = 0
    wts = _prepare(layer, cfg, norm_attn, w_in, mla_q_norm, w_uq, mla_kv_norm, w_ukv, mla_qn_nope, mla_qn_rope,
                   mla_kn_nope, mla_kn_rope, fox_qn, fox_kn, fox_fb, w_pa, w_pb, w_o, norm_ffn, w_gate_up, w_down)
    fw = heads * hd
    n_pairs = heads // 2

    xp = x_prompt.reshape(seq, d)
    rows_p = min(PROJ_ROWS, seq)
    qa, ka, vat, qf, kf, vft, lat_p, krt_p, fkt_p, fvt_p, lft_p = _project(xp, jnp.arange(seq), wts, rows_p)
    oa_p = _attention(qa, ka, vat, head_dim=nope, aug_width=rope, aug_per_pair=True, name="attn_mla")
    ob_p = _attention(qf, kf, vft, head_dim=hd, aug_width=FORGET_GROUP, aug_per_pair=False, name="attn_fox")
    y_p = _merge(xp, oa_p, ob_p, wts, rows_p).reshape(batch, seq, d)

    xs = x_sample.reshape(nb, d)
    pos_s = jnp.full((nb,), past_len, jnp.int32)
    qa_s, _, _, qf_s, _, _, lat_s, krt_s, fkt_s, fvt_s, lft_s = _project(xs, pos_s, wts, nb)
    qa3 = qa_s.reshape(nb, n_pairs, 2 * LANES)
    qn = qa3[:, :, :LANES].reshape(nb, 1, fw)
    qr = qa3[:, :, LANES:LANES + 2 * rope].reshape(nb, heads, rope)
    qfn = qf_s.reshape(nb, n_pairs, 2 * LANES)[:, :, :LANES].reshape(nb, 1, fw)
    cache_kr_t = jnp.transpose(cache_mla_krope, (0, 1, 3, 2))
    cache_fk_t = jnp.transpose(cache_fox_k, (0, 1, 3, 4, 2)).reshape(depth, n_phys, fw, page_size)
    cache_fv_t = jnp.transpose(cache_fox_v, (0, 1, 3, 4, 2)).reshape(depth, n_phys, fw, page_size)
    cache_lf_t = jnp.transpose(cache_fox_logf, (0, 1, 3, 2))
    oa_s, ob_s = _sample_attention(qn, qr, qfn, lat_s, krt_s, fkt_s, fvt_s, lft_s, wts,
                                   cache_mla_latent, cache_kr_t, cache_fk_t, cache_fv_t, cache_lf_t, page_table)
    y_s = _merge(xs, oa_s, ob_s, wts, nb).reshape(nb, dec_seq, d)

    def cache_outputs(lat, krt, fkt, fvt, lft, lead):
        n = lat.shape[0]
        shape = lambda *tail: (depth,) + lead + tail
        return (lat.reshape(shape(kv_rank)),
                krt.T.reshape(shape(rope)),
                jnp.transpose(fkt.reshape(heads, hd, n), (2, 0, 1)).reshape(shape(heads, hd)),
                jnp.transpose(fvt.reshape(heads, hd, n), (2, 0, 1)).reshape(shape(heads, hd)),
                lft.T.reshape(shape(heads)))

    return (y_p, y_s) + cache_outputs(lat_p, krt_p, fkt_p, fvt_p, lft_p, (batch, seq)) \
        + cache_outputs(lat_s, krt_s, fkt_s, fvt_s, lft_s, (nb, dec_seq))
```
